```python
import math
import jax, jax.numpy as jnp
from jax import lax
import numpy as np

D_MODEL = 1024
BATCH = 2
SEQ = 8192
DEPTH = 2

EPS = 1e-6
N_BRANCH = 3
BRANCH_W = 1024
W_A = BRANCH_W
LRU_BLOCKS = 4
LRU_BLOCK = W_A // LRU_BLOCKS
CONV_W = 4
LRU_C = 8.0
W_B = BRANCH_W
RWKV_HEAD = 64
RWKV_HEADS = W_B // RWKV_HEAD
DECAY_LORA = 64
ICLR_LORA = 64
RWKV_GN_EPS = 64e-5
W_C = BRANCH_W
RET_HEADS = 4
RET_HEAD = W_C // RET_HEADS
RET_CHUNK = 128
ROPE_BASE = 10000.0
RET_GN_EPS = 1e-5
B_COLS = 4 * W_B + DECAY_LORA + ICLR_LORA
C_COLS = 4 * W_C
OFF_GA = W_A
OFF_B = 2 * W_A
OFF_C = OFF_B + B_COLS
OFF_M = OFF_C + C_COLS
N_IN = OFF_M + N_BRANCH * D_MODEL

kernel_name = "hybrid_rglru_rwkv7_retention_gated_merge"


def rms_norm(x, w):
    xf = x.astype(jnp.float32)
    y = xf * lax.rsqrt(jnp.mean(xf * xf, axis=-1, keepdims=True) + EPS)
    return (y * w).astype(x.dtype)


def head_norm(y, eps):
    mu = jnp.mean(y, axis=-1, keepdims=True)
    var = jnp.mean(jnp.square(y - mu), axis=-1, keepdims=True)
    return (y - mu) * lax.rsqrt(var + eps)


def rglru_branch(xa, conv_w, conv_b, gate_w, gate_b, lam):
    bsz, s, _ = xa.shape
    xf = xa.astype(jnp.float32)
    xp = jnp.pad(xf, ((0, 0), (CONV_W - 1, 0), (0, 0)))
    conv = conv_b + sum(xp[:, j:j + s] * conv_w[j] for j in range(CONV_W))
    xb = conv.reshape(bsz, s, LRU_BLOCKS, LRU_BLOCK)
    g = jnp.einsum("bsnc,gncd->gbsnd", xb, gate_w).reshape(2, bsz, s, W_A) + gate_b[:, None, None, :]
    r = jax.nn.sigmoid(g[0])
    i = jax.nn.sigmoid(g[1])
    log_a = -LRU_C * r * jax.nn.softplus(-lam)
    a = jnp.exp(log_a)
    mult = jnp.sqrt(jnp.maximum(-jnp.expm1(2.0 * log_a), 0.0))
    mult = jnp.where((jnp.arange(s) == 0)[None, :, None], 1.0, mult)
    b = mult * i * conv

    def combine(left, right):
        a1, b1 = left
        a2, b2 = right
        return a1 * a2, a2 * b1 + b2

    _, h = lax.associative_scan(combine, (a, b), axis=1)
    return h


def rwkv7_branch(pb, mu, w0, w2, a0, a2, k_k, k_a, r_k, lnx_w, lnx_b):
    bsz, s, _ = pb.shape
    pf = pb.astype(jnp.float32)
    prev = jnp.pad(pf, ((0, 0), (1, 0), (0, 0)))[:, :s]
    pf = pf + (prev - pf) * mu
    r, k, v, g, wl, al = jnp.split(pf, [W_B, 2 * W_B, 3 * W_B, 4 * W_B, 4 * W_B + DECAY_LORA], axis=-1)
    w_log = -jax.nn.softplus(-(w0 + jnp.tanh(wl) @ w2)) - 0.5
    decay = jnp.exp(-jnp.exp(w_log))
    a = jax.nn.sigmoid(a0 + al @ a2)
    hd = lambda t: t.reshape(bsz, s, RWKV_HEADS, RWKV_HEAD)
    kk = hd(k * k_k)
    kk = kk / jnp.maximum(jnp.linalg.norm(kk, axis=-1, keepdims=True), 1e-12)
    k = k * (1.0 + (a - 1.0) * k_a)
    rh, kh, vh, wh, ah = hd(r), hd(k), hd(v), hd(decay), hd(a)

    def step(state, inp):
        r_t, w_t, k_t, v_t, kk_t, a_t = inp
        sa = jnp.einsum("bhvk,bhk->bhv", state, -kk_t)
        state = (state * w_t[:, :, None, :]
                 + sa[..., None] * (kk_t * a_t)[:, :, None, :]
                 + v_t[..., None] * k_t[:, :, None, :])
        y = jnp.einsum("bhvk,bhk->bhv", state, r_t)
        return state, y

    tm = lambda t: jnp.moveaxis(t, 1, 0)
    s0 = jnp.zeros((bsz, RWKV_HEADS, RWKV_HEAD, RWKV_HEAD), jnp.float32)
    _, y = lax.scan(step, s0, (tm(rh), tm(wh), tm(kh), tm(vh), tm(kk), tm(ah)))
    y = jnp.moveaxis(y, 0, 1)
    y = head_norm(y, RWKV_GN_EPS).reshape(bsz, s, W_B) * lnx_w + lnx_b
    bonus = (jnp.sum(rh * kh * r_k, axis=-1, keepdims=True) * vh).reshape(bsz, s, W_B)
    return (y + bonus) * jax.nn.silu(g)


def rotary(t, pos):
    half = t.shape[-1] // 2
    inv_freq = 1.0 / (ROPE_BASE ** jnp.linspace(0.0, 1.0, half, dtype=jnp.float32))
    ang = pos[:, None].astype(jnp.float32) * inv_freq
    cos = jnp.cos(ang)[None, :, None, :]
    sin = jnp.sin(ang)[None, :, None, :]
    t1, t2 = t[..., :half], t[..., half:]
    return jnp.concatenate([t1 * cos - t2 * sin, t1 * sin + t2 * cos], axis=-1)


def retention_branch(pc):
    bsz, s, _ = pc.shape
    nc = s // RET_CHUNK
    q, k, v, g = jnp.split(pc.astype(jnp.float32), 4, axis=-1)
    pos = jnp.arange(s)
    hd = lambda t: t.reshape(bsz, s, RET_HEADS, RET_HEAD)
    q = rotary(hd(q), pos)
    k = rotary(hd(k), pos) * (RET_HEAD ** -0.5)
    v = hd(v)
    log_g = jnp.log1p(-jnp.exp2(-5.0 - jnp.arange(RET_HEADS, dtype=jnp.float32)))
    idx = jnp.arange(RET_CHUNK)
    rel = idx[:, None] - idx[None, :]
    mask = jnp.where(rel >= 0, jnp.exp(log_g[:, None, None] * jnp.maximum(rel, 0)), 0.0)
    ch = lambda t: t.reshape(bsz, nc, RET_CHUNK, RET_HEADS, RET_HEAD)
    qc, kc, vc = ch(q), ch(k), ch(v)
    scores = jnp.einsum("bnihd,bnjhd->bnhij", qc, kc) * mask
    inner = jnp.einsum("bnhij,bnjhd->bnihd", scores, vc)
    q_decay = jnp.exp(log_g[None, :] * (idx[:, None] + 1.0))[None, :, :, None]
    k_decay = jnp.exp(log_g[None, :] * (RET_CHUNK - 1.0 - idx[:, None]))[None, :, :, None]
    chunk_decay = jnp.exp(log_g * RET_CHUNK)[None, :, None, None]

    def step(state, inp):
        qn, kn, vn = inp
        cross = jnp.einsum("bihd,bhde->bihe", qn * q_decay, state)
        state = state * chunk_decay + jnp.einsum("bjhd,bjhe->bhde", kn * k_decay, vn)
        return state, cross

    r0 = jnp.zeros((bsz, RET_HEADS, RET_HEAD, RET_HEAD), jnp.float32)
    tm = lambda t: jnp.moveaxis(t, 1, 0)
    _, cross = lax.scan(step, r0, (tm(qc), tm(kc), tm(vc)))
    out = (inner + jnp.moveaxis(cross, 0, 1)).reshape(bsz, s, RET_HEADS, RET_HEAD)
    out = head_norm(out, RET_GN_EPS).reshape(bsz, s, W_C)
    return out * jax.nn.silu(g)


def hybrid_layer(x, norm_w, w_in, b_merge, conv_w, conv_b, lru_gate_w, lru_gate_b, lru_lambda,
                 shift_mu, decay_w0, decay_w2, iclr_a0, iclr_a2, k_k, k_a, r_k, lnx_w, lnx_b,
                 w_branch, w_out):
    bsz, s, _ = x.shape
    h = rms_norm(x, norm_w)
    p = h @ w_in
    xa, ga, pb, pc, pm = jnp.split(p, [OFF_GA, OFF_B, OFF_C, OFF_M], axis=-1)
    y_a = rglru_branch(xa, conv_w, conv_b, lru_gate_w, lru_gate_b, lru_lambda) * jax.nn.silu(ga.astype(jnp.float32))
    y_b = rwkv7_branch(pb, shift_mu, decay_w0, decay_w2, iclr_a0, iclr_a2, k_k, k_a, r_k, lnx_w, lnx_b)
    y_c = retention_branch(pc)
    ys = jnp.stack([y_a, y_b, y_c], axis=0).astype(x.dtype)
    proj = jnp.einsum("nbsw,nwd->bsnd", ys, w_branch)
    gates = jax.nn.sigmoid((pm + b_merge).reshape(bsz, s, N_BRANCH, D_MODEL).astype(jnp.float32))
    merged = jnp.sum(gates * proj, axis=2).astype(x.dtype)
    return x + merged @ w_out


def setup_inputs(seed: int = 0) -> dict:
    key = jax.random.key(seed)
    ks = jax.random.split(key, 22)
    f32 = jnp.float32
    nrm = lambda k, shp: jax.random.normal(k, shp, f32)
    a_c = jax.random.uniform(ks[8], (DEPTH, W_A), f32, 0.9, 0.999)
    a_base = a_c ** (1.0 / LRU_C)
    return {
        "x": nrm(ks[0], (BATCH, SEQ, D_MODEL)),
        "norm_w": 1.0 + 0.02 * nrm(ks[1], (DEPTH, D_MODEL)),
        "w_in": nrm(ks[2], (DEPTH, D_MODEL, N_IN)) * D_MODEL ** -0.5,
        "b_merge": 0.02 * nrm(ks[3], (DEPTH, N_BRANCH * D_MODEL)),
        "conv_w": nrm(ks[4], (DEPTH, CONV_W, W_A)) * CONV_W ** -0.5,
        "conv_b": 0.02 * nrm(ks[5], (DEPTH, W_A)),
        "lru_gate_w": nrm(ks[6], (DEPTH, 2, LRU_BLOCKS, LRU_BLOCK, LRU_BLOCK)) * LRU_BLOCK ** -0.5,
        "lru_gate_b": 0.02 * nrm(ks[7], (DEPTH, 2, W_A)),
        "lru_lambda": jnp.log(a_base) - jnp.log1p(-a_base),
        "shift_mu": jax.random.uniform(ks[9], (DEPTH, B_COLS), f32),
        "decay_w0": jax.random.uniform(ks[10], (DEPTH, W_B), f32, -6.0, -1.0),
        "decay_w2": nrm(ks[11], (DEPTH, DECAY_LORA, W_B)) * 0.5 * DECAY_LORA ** -0.5,
        "iclr_a0": 0.1 * nrm(ks[12], (DEPTH, W_B)),
        "iclr_a2": nrm(ks[13], (DEPTH, ICLR_LORA, W_B)) * ICLR_LORA ** -0.5,
        "k_k": 0.85 + 0.05 * nrm(ks[14], (DEPTH, W_B)),
        "k_a": 1.0 + 0.05 * nrm(ks[15], (DEPTH, W_B)),
        "r_k": 0.1 * nrm(ks[16], (DEPTH, RWKV_HEADS, RWKV_HEAD)),
        "lnx_w": 1.0 + 0.02 * nrm(ks[17], (DEPTH, W_B)),
        "lnx_b": 0.02 * nrm(ks[18], (DEPTH, W_B)),
        "w_branch": nrm(ks[19], (DEPTH, N_BRANCH, BRANCH_W, D_MODEL)) * BRANCH_W ** -0.5,
        "w_out": nrm(ks[20], (DEPTH, D_MODEL, D_MODEL)) * D_MODEL ** -0.5,
        "final_norm_w": 1.0 + 0.02 * nrm(ks[21], (D_MODEL,)),
    }


def reference(x, norm_w, w_in, b_merge, conv_w, conv_b, lru_gate_w, lru_gate_b, lru_lambda,
              shift_mu, decay_w0, decay_w2, iclr_a0, iclr_a2, k_k, k_a, r_k, lnx_w, lnx_b,
              w_branch, w_out, final_norm_w):
    for l in range(DEPTH):
        x = hybrid_layer(x, norm_w[l], w_in[l], b_merge[l], conv_w[l], conv_b[l], lru_gate_w[l],
                         lru_gate_b[l], lru_lambda[l], shift_mu[l], decay_w0[l], decay_w2[l],
                         iclr_a0[l], iclr_a2[l], k_k[l], k_a[l], r_k[l], lnx_w[l], lnx_b[l],
                         w_branch[l], w_out[l])
    return rms_norm(x, final_norm_w)
```

```python
import functools
import math

import jax
import jax.numpy as jnp
from jax import lax
from jax.experimental import pallas as pl
from jax.experimental.pallas import tpu as pltpu

F32 = jnp.float32
BF16 = jnp.bfloat16

D_MODEL = 1024
EPS = 1e-6
BRANCH_W = 1024
LRU_BLOCKS = 4
LRU_BLOCK = BRANCH_W // LRU_BLOCKS
CONV_W = 4
LRU_C = 8.0
RWKV_HEAD = 64
LORA = 64
RWKV_GN_EPS = 64e-5
RET_HEADS = 4
RET_HEAD = BRANCH_W // RET_HEADS
RET_CHUNK = 128
ROPE_BASE = 10000.0
RET_GN_EPS = 1e-5
B_COLS = 4 * BRANCH_W + 2 * LORA
OFF_B = 2 * BRANCH_W
OFF_C = OFF_B + B_COLS
OFF_M = OFF_C + 4 * BRANCH_W
N_IN = OFF_M + 3 * D_MODEL

LANES = 128
RWKV_CHUNK = 64
LRU_TB = 256
VMEM_LIMIT = 48 * 1024 * 1024


def _cparams(n_axes):
    return pltpu.CompilerParams(dimension_semantics=("arbitrary",) * n_axes,
                                vmem_limit_bytes=VMEM_LIMIT)


def _sigmoid(x):
    return 1.0 / (1.0 + jnp.exp(-x))


def _softplus(x):
    return jnp.maximum(x, 0.0) + jnp.log(1.0 + jnp.exp(-jnp.abs(x)))


def _dot(a, b):
    return jnp.dot(a, b, preferred_element_type=F32)


def _dot_nt(a, b):
    return lax.dot_general(a, b, (((1,), (1,)), ((), ())), preferred_element_type=F32)


def _split_hi_lo(x):
    hi = x.astype(BF16)
    lo = (x - hi.astype(F32)).astype(BF16)
    return hi, lo


def _norm_matmul_kernel(x_ref, nw_ref, w_ref, o_ref, h_ref):
    @pl.when(pl.program_id(1) == 0)
    def _():
        x = x_ref[...]
        ms = jnp.mean(x * x, axis=-1, keepdims=True)
        h_ref[...] = (x * lax.rsqrt(ms + EPS) * nw_ref[...]).astype(BF16)

    o_ref[...] = _dot(h_ref[...], w_ref[...])


def _norm_matmul(x2, norm_w, w_bf, tm, tn):
    t, d = x2.shape
    n = w_bf.shape[1]
    return pl.pallas_call(
        _norm_matmul_kernel,
        out_shape=jax.ShapeDtypeStruct((t, n), F32),
        grid=(t // tm, n // tn),
        in_specs=[pl.BlockSpec((tm, d), lambda i, j: (i, 0)),
                  pl.BlockSpec((1, d), lambda i, j: (0, 0)),
                  pl.BlockSpec((d, tn), lambda i, j: (0, j))],
        out_specs=pl.BlockSpec((tm, tn), lambda i, j: (i, j)),
        scratch_shapes=[pltpu.VMEM((tm, d), BF16)],
        compiler_params=_cparams(2),
        name="norm_matmul",
    )(x2, norm_w.reshape(1, d), w_bf)


def _lru_kernel(xa_ref, ga_ref, cw_ref, cb_ref, gw_ref, gb_ref, lam_ref, o_ref,
                xpad_ref, a_ref, b_ref, hs_ref, h_ref):
    tb = xa_ref.shape[0]
    w = xa_ref.shape[1]
    i = pl.program_id(1)

    @pl.when(i == 0)
    def _():
        xpad_ref[0:8, :] = jnp.zeros((8, w), F32)
        h_ref[...] = jnp.zeros_like(h_ref)

    @pl.when(i > 0)
    def _():
        xpad_ref[0:8, :] = xpad_ref[tb:tb + 8, :]

    xpad_ref[8:8 + tb, :] = xa_ref[...]
    conv = jnp.broadcast_to(cb_ref[...], (tb, w))
    for j in range(CONV_W):
        conv = conv + xpad_ref[pl.ds(8 - (CONV_W - 1) + j, tb), :] * cw_ref[j:j + 1, :]

    conv_bf = conv.astype(BF16)
    g0, g1 = [], []
    for n in range(LRU_BLOCKS):
        cn = conv_bf[:, n * LRU_BLOCK:(n + 1) * LRU_BLOCK]
        g0.append(_dot(cn, gw_ref[0, n]))
        g1.append(_dot(cn, gw_ref[1, n]))
    g0 = jnp.concatenate(g0, axis=1) + gb_ref[0:1, :]
    g1 = jnp.concatenate(g1, axis=1) + gb_ref[1:2, :]
    r = _sigmoid(g0)
    ig = _sigmoid(g1)
    log_a = (-LRU_C) * r * _softplus(-lam_ref[...])
    a = jnp.exp(log_a)
    mult = jnp.sqrt(jnp.maximum(-jnp.tanh(log_a) * (a * a + 1.0), 0.0))
    row = lax.broadcasted_iota(jnp.int32, (tb, w), 0)
    mult = jnp.where((row == 0) & (i == 0), 1.0, mult)
    a_ref[...] = a
    b_ref[...] = mult * ig * conv

    def body(g, h):
        base = pl.multiple_of(g * 8, 8)
        a8 = a_ref[pl.ds(base, 8), :]
        b8 = b_ref[pl.ds(base, 8), :]
        rows = []
        for rr in range(8):
            h = a8[rr:rr + 1, :] * h + b8[rr:rr + 1, :]
            rows.append(h)
        hs_ref[pl.ds(base, 8), :] = jnp.concatenate(rows, axis=0)
        return h

    h_ref[...] = lax.fori_loop(0, tb // 8, body, h_ref[...])
    ga = ga_ref[...]
    o_ref[...] = (hs_ref[...] * (ga * _sigmoid(ga))).astype(o_ref.dtype)


def _lru_branch(p_a, conv_w, conv_b, gate_w_bf, gate_b, lam, bsz, s):
    w = BRANCH_W
    tb = min(LRU_TB, s)
    nb = s // tb
    return pl.pallas_call(
        _lru_kernel,
        out_shape=jax.ShapeDtypeStruct((bsz * s, w), BF16),
        grid=(bsz, nb),
        in_specs=[pl.BlockSpec((tb, w), lambda b, i: (b * nb + i, 0)),
                  pl.BlockSpec((tb, w), lambda b, i: (b * nb + i, 1)),
                  pl.BlockSpec((CONV_W, w), lambda b, i: (0, 0)),
                  pl.BlockSpec((1, w), lambda b, i: (0, 0)),
                  pl.BlockSpec((2, LRU_BLOCKS, LRU_BLOCK, LRU_BLOCK), lambda b, i: (0, 0, 0, 0)),
                  pl.BlockSpec((2, w), lambda b, i: (0, 0)),
                  pl.BlockSpec((1, w), lambda b, i: (0, 0))],
        out_specs=pl.BlockSpec((tb, w), lambda b, i: (b * nb + i, 0)),
        scratch_shapes=[pltpu.VMEM((tb + 8, w), F32), pltpu.VMEM((tb, w), F32),
                        pltpu.VMEM((tb, w), F32), pltpu.VMEM((tb, w), F32),
                        pltpu.VMEM((1, w), F32)],
        compiler_params=_cparams(2),
        name="rglru",
    )(p_a, p_a, conv_w, conv_b.reshape(1, w), gate_w_bf, gate_b, lam.reshape(1, w))


def _seg64_sum(x, bd_ones):
    hi, lo = _split_hi_lo(x)
    out = []
    for j in range(x.shape[1] // LANES):
        sl = slice(j * LANES, (j + 1) * LANES)
        out.append(_dot(hi[:, sl], bd_ones) + _dot(lo[:, sl], bd_ones))
    return jnp.concatenate(out, axis=1)


def _shift_rows(x, prev_row):
    rolled = pltpu.roll(x, 1, 0)
    row = lax.broadcasted_iota(jnp.int32, x.shape, 0)
    return jnp.where(row == 0, jnp.broadcast_to(prev_row, x.shape), rolled)


def _rwkv_kernel(r_ref, k_ref, v_ref, g_ref, lo_ref, mu_ref, mulo_ref, w2a_ref, w0_ref, a0_ref,
                 kk_ref, ka_ref, rk_ref, lnw_ref, lnb_ref, o_ref, prev_ref, prevlo_ref, state_ref):
    L = r_ref.shape[0]
    w = r_ref.shape[1]
    n_pairs = w // LANES
    i = pl.program_id(1)

    @pl.when(i == 0)
    def _():
        prev_ref[...] = jnp.zeros_like(prev_ref)
        prevlo_ref[...] = jnp.zeros_like(prevlo_ref)
        state_ref[...] = jnp.zeros_like(state_ref)

    def token_shift(x, prev_row, mu_row):
        return x + (_shift_rows(x, prev_row) - x) * mu_row

    r_raw, k_raw, v_raw, g_raw, lo_raw = r_ref[...], k_ref[...], v_ref[...], g_ref[...], lo_ref[...]
    r = token_shift(r_raw, prev_ref[0:1, :], mu_ref[0:1, :])
    k = token_shift(k_raw, prev_ref[1:2, :], mu_ref[1:2, :])
    v = token_shift(v_raw, prev_ref[2:3, :], mu_ref[2:3, :])
    g = token_shift(g_raw, prev_ref[3:4, :], mu_ref[3:4, :])
    lo = token_shift(lo_raw, prevlo_ref[...], mulo_ref[...])
    prev_ref[0:1, :] = r_raw[L - 1:L, :]
    prev_ref[1:2, :] = k_raw[L - 1:L, :]
    prev_ref[2:3, :] = v_raw[L - 1:L, :]
    prev_ref[3:4, :] = g_raw[L - 1:L, :]
    prevlo_ref[...] = lo_raw[L - 1:L, :]

    lane_lo = lax.broadcasted_iota(jnp.int32, lo.shape, 1)
    lo_in = jnp.where(lane_lo < LORA, jnp.tanh(lo), lo).astype(BF16)
    dwa = _dot(lo_in, w2a_ref[...])
    w_log = -_softplus(-(w0_ref[...] + dwa[:, :w])) - 0.5
    ld = -jnp.exp(w_log)
    a = _sigmoid(a0_ref[...] + dwa[:, w:])

    ri = lax.broadcasted_iota(jnp.int32, (LANES, LANES), 0)
    ci = lax.broadcasted_iota(jnp.int32, (LANES, LANES), 1)
    bd_ones = ((ri >> 6) == (ci >> 6)).astype(BF16)

    kk = k * kk_ref[...]
    kk = kk / jnp.maximum(jnp.sqrt(_seg64_sum(kk * kk, bd_ones)), 1e-12)
    kv = k * (1.0 + (a - 1.0) * ka_ref[...])
    b = kk * a

    rl = lax.broadcasted_iota(jnp.int32, (L, L), 0)
    cl = lax.broadcasted_iota(jnp.int32, (L, L), 1)
    tril_ones = (cl <= rl).astype(BF16)
    ld_hi, ld_lo = _split_hi_lo(ld)
    cum = _dot(tril_ones, ld_hi) + _dot(tril_ones, ld_lo)
    cum_last = cum[L - 1:L, :]
    p_inc = jnp.exp(cum)
    p_exc = jnp.exp(cum - ld)
    p_inv = jnp.exp(-cum)
    p_hat = jnp.exp(cum_last - cum)
    p_last = jnp.exp(cum_last)

    kap_t = kk * p_exc
    r_t = r * p_inc
    b_t = b * p_inv
    k_t = kv * p_inv
    b_h = b * p_hat
    k_h = kv * p_hat

    lane = lax.broadcasted_iota(jnp.int32, (L, LANES), 1)
    first = lane < RWKV_HEAD

    def stack2(x):
        return jnp.concatenate([jnp.where(first, x, 0.0), jnp.where(first, 0.0, x)], axis=0)

    strict = (ci & (L - 1)) < (ri & (L - 1))
    incl = (ci & (L - 1)) <= (ri & (L - 1))
    eye = (ri == ci).astype(F32)

    ys = []
    for j in range(n_pairs):
        sl = slice(j * LANES, (j + 1) * LANES)
        kp = stack2(kap_t[:, sl]).astype(BF16)
        rp = stack2(r_t[:, sl]).astype(BF16)
        bp = stack2(b_t[:, sl]).astype(BF16)
        kkp = stack2(k_t[:, sl]).astype(BF16)
        bh = stack2(b_h[:, sl]).astype(BF16)
        kh = stack2(k_h[:, sl]).astype(BF16)
        v2 = stack2(v[:, sl])
        v2_bf = v2.astype(BF16)

        a_ab = jnp.where(strict, _dot_nt(kp, bp), 0.0)
        a_ak = jnp.where(strict, _dot_nt(kp, kkp), 0.0)
        r_b = jnp.where(incl, _dot_nt(rp, bp), 0.0)
        r_k = jnp.where(incl, _dot_nt(rp, kkp), 0.0)

        nmat = -a_ab
        tinv = eye + nmat
        pw = nmat
        for _ in range(int(math.log2(L)) - 1):
            pw = _dot(pw, pw)
            tinv = tinv + _dot(tinv, pw)

        s_bd = state_ref[j]
        x0k = _dot_nt(kp, s_bd)
        x0r = _dot_nt(rp, s_bd)
        rhs = -(x0k + _dot(a_ak, v2_bf))
        u2 = _dot(tinv, rhs)
        u2_bf = u2.astype(BF16)
        y2 = x0r + _dot(r_b, u2_bf) + _dot(r_k, v2_bf)
        state_ref[j] = (s_bd * p_last[:, sl]
                        + _dot(u2.T, bh) + _dot(v2.T, kh))
        ys.append(y2[0:L, :] + y2[L:2 * L, :])
    y = jnp.concatenate(ys, axis=1)

    mean = _seg64_sum(y, bd_ones) * (1.0 / RWKV_HEAD)
    d = y - mean
    var = _seg64_sum(d * d, bd_ones) * (1.0 / RWKV_HEAD)
    yn = d * lax.rsqrt(var + RWKV_GN_EPS) * lnw_ref[...] + lnb_ref[...]
    bonus = _seg64_sum(r * kv * rk_ref[...], bd_ones) * v
    o_ref[...] = ((yn + bonus) * (g * _sigmoid(g))).astype(o_ref.dtype)


def _rwkv_branch(p_b, mu, w2a_bf, w0, a0, k_k, k_a, r_k, lnx_w, lnx_b, bsz, s):
    w = BRANCH_W
    L = RWKV_CHUNK
    nb = s // L
    row = lambda x: x.reshape(1, -1)
    mu4 = mu[:4 * w].reshape(4, w)
    mulo = mu[4 * w:].reshape(1, 2 * LORA)
    tok = lambda c: pl.BlockSpec((L, w), lambda b, i, c=c: (b * nb + i, c))
    par = lambda shp: pl.BlockSpec(shp, lambda b, i: (0,) * len(shp))
    return pl.pallas_call(
        _rwkv_kernel,
        out_shape=jax.ShapeDtypeStruct((bsz * s, w), BF16),
        grid=(bsz, nb),
        in_specs=[tok(0), tok(1), tok(2), tok(3),
                  pl.BlockSpec((L, 2 * LORA), lambda b, i: (b * nb + i, 4 * w // (2 * LORA))),
                  par((4, w)), par((1, 2 * LORA)), par((2 * LORA, 2 * w)),
                  par((1, w)), par((1, w)), par((1, w)), par((1, w)), par((1, w)),
                  par((1, w)), par((1, w))],
        out_specs=pl.BlockSpec((L, w), lambda b, i: (b * nb + i, 0)),
        scratch_shapes=[pltpu.VMEM((4, w), F32), pltpu.VMEM((1, 2 * LORA), F32),
                        pltpu.VMEM((w // LANES, LANES, LANES), F32)],
        compiler_params=_cparams(2),
        name="rwkv7",
    )(p_b, p_b, p_b, p_b, p_b, mu4, mulo, w2a_bf, row(w0), row(a0), row(k_k), row(k_a),
      row(r_k), row(lnx_w), row(lnx_b))


def _ret_kernel(q_ref, k_ref, v_ref, g_ref, cos_ref, sin_ref, o_ref, state_ref):
    C = q_ref.shape[0]
    dh = RET_HEAD
    half = dh // 2
    i = pl.program_id(1)

    @pl.when(i == 0)
    def _():
        state_ref[...] = jnp.zeros_like(state_ref)

    cos = cos_ref[...]
    sin = sin_ref[...]
    ri = lax.broadcasted_iota(jnp.int32, (C, C), 0)
    ci = lax.broadcasted_iota(jnp.int32, (C, C), 1)
    rel = (ri - ci).astype(F32)
    rowf = lax.broadcasted_iota(jnp.int32, (C, dh), 0).astype(F32)

    def rot(t):
        t1, t2 = t[:, :half], t[:, half:]
        return jnp.concatenate([t1 * cos - t2 * sin, t1 * sin + t2 * cos], axis=1)

    outs = []
    for h in range(RET_HEADS):
        sl = slice(h * dh, (h + 1) * dh)
        log_g = math.log1p(-(2.0 ** (-5.0 - h)))
        q = rot(q_ref[:, sl])
        k = rot(k_ref[:, sl]) * (dh ** -0.5)
        v = v_ref[:, sl]
        v_bf = v.astype(BF16)
        mask = jnp.where(rel >= 0.0, jnp.exp(log_g * jnp.maximum(rel, 0.0)), 0.0)
        scores = _dot_nt(q.astype(BF16), k.astype(BF16)) * mask
        inner = _dot(scores.astype(BF16), v_bf)
        q_dec = jnp.exp(log_g * (rowf + 1.0))
        k_dec = jnp.exp(log_g * (C - 1.0 - rowf))
        st = state_ref[h]
        cross = _dot((q * q_dec).astype(BF16), st.astype(BF16))
        state_ref[h] = st * math.exp(log_g * C) + _dot((k * k_dec).T.astype(BF16), v_bf)
        out = inner + cross
        mu = jnp.mean(out, axis=-1, keepdims=True)
        dlt = out - mu
        var = jnp.mean(dlt * dlt, axis=-1, keepdims=True)
        outs.append(dlt * lax.rsqrt(var + RET_GN_EPS))
    out = jnp.concatenate(outs, axis=1)
    g = g_ref[...]
    o_ref[...] = (out * (g * _sigmoid(g))).astype(o_ref.dtype)


def _ret_branch(p_c, cos_t, sin_t, bsz, s):
    w = BRANCH_W
    C = RET_CHUNK
    nb = s // C
    tok = lambda c: pl.BlockSpec((C, w), lambda b, i, c=c: (b * nb + i, c))
    tab = pl.BlockSpec((C, RET_HEAD // 2), lambda b, i: (i, 0))
    return pl.pallas_call(
        _ret_kernel,
        out_shape=jax.ShapeDtypeStruct((bsz * s, w), BF16),
        grid=(bsz, nb),
        in_specs=[tok(0), tok(1), tok(2), tok(3), tab, tab],
        out_specs=pl.BlockSpec((C, w), lambda b, i: (b * nb + i, 0)),
        scratch_shapes=[pltpu.VMEM((RET_HEADS, RET_HEAD, RET_HEAD), F32)],
        compiler_params=_cparams(2),
        name="retention",
    )(p_c, p_c, p_c, p_c, cos_t, sin_t)


def _merge_kernel(x_ref, ya_ref, yb_ref, yc_ref, pm_ref, bm_ref, wb_ref, wo_ref, fw_ref, o_ref,
                  *, final_norm):
    d = x_ref.shape[1]
    merged = None
    for n, y_ref in enumerate((ya_ref, yb_ref, yc_ref)):
        proj = _dot(y_ref[...], wb_ref[n])
        gate = _sigmoid(pm_ref[:, n * d:(n + 1) * d] + bm_ref[:, n * d:(n + 1) * d])
        merged = gate * proj if merged is None else merged + gate * proj
    out = x_ref[...] + _dot(merged.astype(BF16), wo_ref[...])
    if final_norm:
        ms = jnp.mean(out * out, axis=-1, keepdims=True)
        out = out * lax.rsqrt(ms + EPS) * fw_ref[...]
    o_ref[...] = out


def _merge(x2, y_a, y_b, y_c, p_m, b_merge, w_branch_bf, w_out_bf, final_w, final_norm, tm):
    t, d = x2.shape
    w = BRANCH_W
    rows = lambda n: pl.BlockSpec((tm, n), lambda i: (i, 0))
    return pl.pallas_call(
        functools.partial(_merge_kernel, final_norm=final_norm),
        out_shape=jax.ShapeDtypeStruct((t, d), F32),
        grid=(t // tm,),
        in_specs=[rows(d), rows(w), rows(w), rows(w), rows(3 * d),
                  pl.BlockSpec((1, 3 * d), lambda i: (0, 0)),
                  pl.BlockSpec((3, w, d), lambda i: (0, 0, 0)),
                  pl.BlockSpec((d, d), lambda i: (0, 0)),
                  pl.BlockSpec((1, d), lambda i: (0, 0))],
        out_specs=rows(d),
        compiler_params=_cparams(1),
        name="merge",
    )(x2, y_a, y_b, y_c, p_m, b_merge.reshape(1, 3 * d), w_branch_bf, w_out_bf,
      final_w.reshape(1, d))


def _rope_tables(s):
    half = RET_HEAD // 2
    inv_freq = 1.0 / (ROPE_BASE ** jnp.linspace(0.0, 1.0, half, dtype=F32))
    ang = jnp.arange(s)[:, None].astype(F32) * inv_freq
    return jnp.cos(ang), jnp.sin(ang)


def _lora_block(decay_w2, iclr_a2):
    z = jnp.zeros((LORA, BRANCH_W), F32)
    return jnp.concatenate([jnp.concatenate([decay_w2, z], axis=1),
                            jnp.concatenate([z, iclr_a2], axis=1)], axis=0).astype(BF16)


def kernel(x, norm_w, w_in, b_merge, conv_w, conv_b, lru_gate_w, lru_gate_b, lru_lambda, shift_mu,
           decay_w0, decay_w2, iclr_a0, iclr_a2, k_k, k_a, r_k, lnx_w, lnx_b, w_branch, w_out,
           final_norm_w):
    bsz, s, d = x.shape
    depth = norm_w.shape[0]
    t = bsz * s
    tm = min(512, t)
    cos_t, sin_t = _rope_tables(s)
    x2 = x.reshape(t, d)
    for l in range(depth):
        w_bf = w_in[l].astype(BF16)
        p_a = _norm_matmul(x2, norm_w[l], w_bf[:, :OFF_B], tm, 1024)
        p_b = _norm_matmul(x2, norm_w[l], w_bf[:, OFF_B:OFF_C], tm, B_COLS // 3)
        p_c = _norm_matmul(x2, norm_w[l], w_bf[:, OFF_C:OFF_M], tm, 1024)
        p_m = _norm_matmul(x2, norm_w[l], w_bf[:, OFF_M:], tm, 1024)
        y_a = _lru_branch(p_a, conv_w[l], conv_b[l], lru_gate_w[l].astype(BF16), lru_gate_b[l],
                          lru_lambda[l], bsz, s)
        y_b = _rwkv_branch(p_b, shift_mu[l], _lora_block(decay_w2[l], iclr_a2[l]), decay_w0[l],
                           iclr_a0[l], k_k[l], k_a[l], r_k[l].reshape(-1), lnx_w[l], lnx_b[l],
                           bsz, s)
        y_c = _ret_branch(p_c, cos_t, sin_t, bsz, s)
        x2 = _merge(x2, y_a, y_b, y_c, p_m, b_merge[l], w_branch[l].astype(BF16),
                    w_out[l].astype(BF16), final_norm_w, l == depth - 1, tm)
    return x2.reshape(bsz, s, d)
```

```python
import functools
import math

import jax
import jax.numpy as jnp
from jax import lax
from jax.experimental import pallas as pl
from jax.experimental.pallas import tpu as pltpu

F32 = jnp.float32
BF16 = jnp.bfloat16

D_MODEL = 1024
EPS = 1e-6
BRANCH_W = 1024
LRU_BLOCKS = 4
LRU_BLOCK = BRANCH_W // LRU_BLOCKS
CONV_W = 4
LRU_C = 8.0
RWKV_HEAD = 64
LORA = 64
RWKV_GN_EPS = 64e-5
RET_HEADS = 4
RET_HEAD = BRANCH_W // RET_HEADS
RET_CHUNK = 128
ROPE_BASE = 10000.0
RET_GN_EPS = 1e-5
B_COLS = 4 * BRANCH_W + 2 * LORA
OFF_B = 2 * BRANCH_W
OFF_C = OFF_B + B_COLS
OFF_M = OFF_C + 4 * BRANCH_W
N_IN = OFF_M + 3 * D_MODEL
P_OFF_C = 0
P_OFF_A = 4 * BRANCH_W
P_OFF_M = P_OFF_A + 2 * BRANCH_W
P_OFF_B = P_OFF_M + 3 * D_MODEL

LANES = 128
RWKV_CHUNK = 64
LRU_TB = 256
PROJ_TN = 1920
VMEM_LIMIT = 48 * 1024 * 1024


def _cparams(n_axes):
    return pltpu.CompilerParams(dimension_semantics=("arbitrary",) * n_axes,
                                vmem_limit_bytes=VMEM_LIMIT)


def _sigmoid(x):
    return 1.0 / (1.0 + jnp.exp(-x))


def _softplus(x):
    return jnp.maximum(x, 0.0) + jnp.log(1.0 + jnp.exp(-jnp.abs(x)))


def _dot(a, b):
    return jnp.dot(a, b, preferred_element_type=F32)


def _dot_nt(a, b):
    return lax.dot_general(a, b, (((1,), (1,)), ((), ())), preferred_element_type=F32)


def _split_hi_lo(x):
    hi = x.astype(BF16)
    lo = (x - hi.astype(F32)).astype(BF16)
    return hi, lo


def _norm_matmul_kernel(x_ref, nw_ref, w_ref, o_ref, h_ref):
    @pl.when(pl.program_id(1) == 0)
    def _():
        x = x_ref[...]
        ms = jnp.mean(x * x, axis=-1, keepdims=True)
        h_ref[...] = (x * lax.rsqrt(ms + EPS) * nw_ref[...]).astype(BF16)

    o_ref[...] = _dot(h_ref[...], w_ref[...]).astype(o_ref.dtype)


def _norm_matmul(x2, norm_w, w_bf, tm, tn):
    t, d = x2.shape
    n = w_bf.shape[1]
    return pl.pallas_call(
        _norm_matmul_kernel,
        out_shape=jax.ShapeDtypeStruct((t, n), BF16),
        grid=(t // tm, n // tn),
        in_specs=[pl.BlockSpec((tm, d), lambda i, j: (i, 0)),
                  pl.BlockSpec((1, d), lambda i, j: (0, 0)),
                  pl.BlockSpec((d, tn), lambda i, j: (0, j))],
        out_specs=pl.BlockSpec((tm, tn), lambda i, j: (i, j)),
        scratch_shapes=[pltpu.VMEM((tm, d), BF16)],
        compiler_params=_cparams(2),
        name="norm_matmul",
    )(x2, norm_w.reshape(1, d), w_bf)


def _lru_kernel(xa_ref, ga_ref, cw_ref, cb_ref, gw_ref, gb_ref, lam_ref, o_ref,
                xpad_ref, a_ref, b_ref, hs_ref, h_ref):
    tb = xa_ref.shape[0]
    w = xa_ref.shape[1]
    i = pl.program_id(1)

    @pl.when(i == 0)
    def _():
        xpad_ref[0:8, :] = jnp.zeros((8, w), F32)
        h_ref[...] = jnp.zeros_like(h_ref)

    @pl.when(i > 0)
    def _():
        xpad_ref[0:8, :] = xpad_ref[tb:tb + 8, :]

    xpad_ref[8:8 + tb, :] = xa_ref[...].astype(F32)
    conv = jnp.broadcast_to(cb_ref[...], (tb, w))
    for j in range(CONV_W):
        conv = conv + xpad_ref[pl.ds(8 - (CONV_W - 1) + j, tb), :] * cw_ref[j:j + 1, :]

    conv_bf = conv.astype(BF16)
    g0, g1 = [], []
    for n in range(LRU_BLOCKS):
        cn = conv_bf[:, n * LRU_BLOCK:(n + 1) * LRU_BLOCK]
        g0.append(_dot(cn, gw_ref[0, n]))
        g1.append(_dot(cn, gw_ref[1, n]))
    g0 = jnp.concatenate(g0, axis=1) + gb_ref[0:1, :]
    g1 = jnp.concatenate(g1, axis=1) + gb_ref[1:2, :]
    r = _sigmoid(g0)
    ig = _sigmoid(g1)
    log_a = (-LRU_C) * r * _softplus(-lam_ref[...])
    a = jnp.exp(log_a)
    mult = jnp.sqrt(jnp.maximum(-jnp.tanh(log_a) * (a * a + 1.0), 0.0))
    row = lax.broadcasted_iota(jnp.int32, (tb, w), 0)
    mult = jnp.where((row == 0) & (i == 0), 1.0, mult)
    a_ref[...] = a
    b_ref[...] = mult * ig * conv

    def body(g, h):
        base = pl.multiple_of(g * 8, 8)
        a8 = a_ref[pl.ds(base, 8), :]
        b8 = b_ref[pl.ds(base, 8), :]
        rows = []
        for rr in range(8):
            h = a8[rr:rr + 1, :] * h + b8[rr:rr + 1, :]
            rows.append(h)
        hs_ref[pl.ds(base, 8), :] = jnp.concatenate(rows, axis=0)
        return h

    h_ref[...] = lax.fori_loop(0, tb // 8, body, h_ref[...])
    ga = ga_ref[...].astype(F32)
    o_ref[...] = (hs_ref[...] * (ga * _sigmoid(ga))).astype(o_ref.dtype)


def _lru_branch(p, conv_w, conv_b, gate_w_bf, gate_b, lam, bsz, s):
    w = BRANCH_W
    tb = min(LRU_TB, s)
    nb = s // tb
    return pl.pallas_call(
        _lru_kernel,
        out_shape=jax.ShapeDtypeStruct((bsz * s, w), BF16),
        grid=(bsz, nb),
        in_specs=[pl.BlockSpec((tb, w), lambda b, i: (b * nb + i, P_OFF_A // w)),
                  pl.BlockSpec((tb, w), lambda b, i: (b * nb + i, P_OFF_A // w + 1)),
                  pl.BlockSpec((CONV_W, w), lambda b, i: (0, 0)),
                  pl.BlockSpec((1, w), lambda b, i: (0, 0)),
                  pl.BlockSpec((2, LRU_BLOCKS, LRU_BLOCK, LRU_BLOCK), lambda b, i: (0, 0, 0, 0)),
                  pl.BlockSpec((2, w), lambda b, i: (0, 0)),
                  pl.BlockSpec((1, w), lambda b, i: (0, 0))],
        out_specs=pl.BlockSpec((tb, w), lambda b, i: (b * nb + i, 0)),
        scratch_shapes=[pltpu.VMEM((tb + 8, w), F32), pltpu.VMEM((tb, w), F32),
                        pltpu.VMEM((tb, w), F32), pltpu.VMEM((tb, w), F32),
                        pltpu.VMEM((1, w), F32)],
        compiler_params=_cparams(2),
        name="rglru",
    )(p, p, conv_w, conv_b.reshape(1, w), gate_w_bf, gate_b, lam.reshape(1, w))


def _seg64_sum(x, bd_ones):
    hi, lo = _split_hi_lo(x)
    out = []
    for j in range(x.shape[1] // LANES):
        sl = slice(j * LANES, (j + 1) * LANES)
        out.append(_dot(hi[:, sl], bd_ones) + _dot(lo[:, sl], bd_ones))
    return jnp.concatenate(out, axis=1)


def _shift_rows(x, prev_row):
    rolled = pltpu.roll(x, 1, 0)
    row = lax.broadcasted_iota(jnp.int32, x.shape, 0)
    return jnp.where(row == 0, jnp.broadcast_to(prev_row, x.shape), rolled)


def _rwkv_kernel(r_ref, k_ref, v_ref, g_ref, lo_ref, mu_ref, mulo_ref, w2a_ref, w0_ref, a0_ref,
                 kk_ref, ka_ref, rk_ref, lnw_ref, lnb_ref, o_ref, prev_ref, prevlo_ref, state_ref):
    L = r_ref.shape[0]
    w = r_ref.shape[1]
    n_pairs = w // LANES
    i = pl.program_id(1)

    @pl.when(i == 0)
    def _():
        prev_ref[...] = jnp.zeros_like(prev_ref)
        prevlo_ref[...] = jnp.zeros_like(prevlo_ref)
        state_ref[...] = jnp.zeros_like(state_ref)

    def token_shift(x, prev_row, mu_row):
        return x + (_shift_rows(x, prev_row) - x) * mu_row

    r_raw, k_raw, v_raw, g_raw, lo_raw = (ref[...].astype(F32)
                                          for ref in (r_ref, k_ref, v_ref, g_ref, lo_ref))
    r = token_shift(r_raw, prev_ref[0:1, :], mu_ref[0:1, :])
    k = token_shift(k_raw, prev_ref[1:2, :], mu_ref[1:2, :])
    v = token_shift(v_raw, prev_ref[2:3, :], mu_ref[2:3, :])
    g = token_shift(g_raw, prev_ref[3:4, :], mu_ref[3:4, :])
    lo = token_shift(lo_raw, prevlo_ref[...], mulo_ref[...])
    prev_ref[0:1, :] = r_raw[L - 1:L, :]
    prev_ref[1:2, :] = k_raw[L - 1:L, :]
    prev_ref[2:3, :] = v_raw[L - 1:L, :]
    prev_ref[3:4, :] = g_raw[L - 1:L, :]
    prevlo_ref[...] = lo_raw[L - 1:L, :]

    lane_lo = lax.broadcasted_iota(jnp.int32, lo.shape, 1)
    lo_in = jnp.where(lane_lo < LORA, jnp.tanh(lo), lo).astype(BF16)
    dwa = _dot(lo_in, w2a_ref[...])
    w_log = -_softplus(-(w0_ref[...] + dwa[:, :w])) - 0.5
    ld = -jnp.exp(w_log)
    a = _sigmoid(a0_ref[...] + dwa[:, w:])

    ri = lax.broadcasted_iota(jnp.int32, (LANES, LANES), 0)
    ci = lax.broadcasted_iota(jnp.int32, (LANES, LANES), 1)
    bd_ones = ((ri >> 6) == (ci >> 6)).astype(BF16)

    kk = k * kk_ref[...]
    kk = kk / jnp.maximum(jnp.sqrt(_seg64_sum(kk * kk, bd_ones)), 1e-12)
    kv = k * (1.0 + (a - 1.0) * ka_ref[...])
    b = kk * a

    rl = lax.broadcasted_iota(jnp.int32, (L, L), 0)
    cl = lax.broadcasted_iota(jnp.int32, (L, L), 1)
    tril_ones = (cl <= rl).astype(BF16)
    ld_hi, ld_lo = _split_hi_lo(ld)
    cum = _dot(tril_ones, ld_hi) + _dot(tril_ones, ld_lo)
    cum_last = cum[L - 1:L, :]
    p_inc = jnp.exp(cum)
    p_exc = jnp.exp(cum - ld)
    p_inv = jnp.exp(-cum)
    p_hat = jnp.exp(cum_last - cum)
    p_last = jnp.exp(cum_last)

    kap_t = kk * p_exc
    r_t = r * p_inc
    b_t = b * p_inv
    k_t = kv * p_inv
    b_h = b * p_hat
    k_h = kv * p_hat

    lane = lax.broadcasted_iota(jnp.int32, (L, LANES), 1)
    first = lane < RWKV_HEAD

    def stack2(x):
        return jnp.concatenate([jnp.where(first, x, 0.0), jnp.where(first, 0.0, x)], axis=0)

    strict = (ci & (L - 1)) < (ri & (L - 1))
    incl = (ci & (L - 1)) <= (ri & (L - 1))
    incl2 = jnp.concatenate([incl, incl], axis=1)
    eye = (ri == ci).astype(F32)
    bf = lambda x: x.astype(BF16)
    P2 = 2 * L

    pair_sl = [slice(j * LANES, (j + 1) * LANES) for j in range(n_pairs)]
    gq = [bf(jnp.concatenate([stack2(kap_t[:, sl]), stack2(r_t[:, sl])], axis=0)) for sl in pair_sl]
    zk = [bf(jnp.concatenate([stack2(b_t[:, sl]), stack2(k_t[:, sl])], axis=0)) for sl in pair_sl]
    zh = [bf(jnp.concatenate([stack2(b_h[:, sl]), stack2(k_h[:, sl])], axis=0)) for sl in pair_sl]
    v2 = [stack2(v[:, sl]) for sl in pair_sl]

    sc = [_dot_nt(g_, z_) for g_, z_ in zip(gq, zk)]
    pw = [jnp.where(strict, -s_[0:P2, 0:P2], 0.0) for s_ in sc]
    a_ak = [bf(jnp.where(strict, s_[0:P2, P2:2 * P2], 0.0)) for s_ in sc]
    r_bk = [bf(jnp.where(incl2, s_[P2:2 * P2, :], 0.0)) for s_ in sc]
    akv = [_dot(a_, bf(v_)) for a_, v_ in zip(a_ak, v2)]

    tinv = [eye + p_ for p_ in pw]
    for _ in range(int(math.log2(L)) - 1):
        pw_bf = [bf(p_) for p_ in pw]
        pw = [_dot(p_, p_) for p_ in pw_bf]
        pw_bf = [bf(p_) for p_ in pw]
        tinv = [t_ + _dot(bf(t_), p_) for t_, p_ in zip(tinv, pw_bf)]

    s_bd = [state_ref[j] for j in range(n_pairs)]
    x0 = [_dot_nt(g_, bf(s_)) for g_, s_ in zip(gq, s_bd)]
    rhs = [-(x_[0:P2, :] + a_) for x_, a_ in zip(x0, akv)]
    u2 = [_dot(bf(t_), bf(r_)) for t_, r_ in zip(tinv, rhs)]
    uv = [jnp.concatenate([u_, v_], axis=0) for u_, v_ in zip(u2, v2)]
    y2 = [x_[P2:2 * P2, :] + _dot(r_, bf(uv_)) for x_, r_, uv_ in zip(x0, r_bk, uv)]
    for j in range(n_pairs):
        state_ref[j] = s_bd[j] * p_last[:, pair_sl[j]] + _dot(bf(uv[j].T), zh[j])
    y = jnp.concatenate([y_[0:L, :] + y_[L:P2, :] for y_ in y2], axis=1)

    mean = _seg64_sum(y, bd_ones) * (1.0 / RWKV_HEAD)
    d = y - mean
    var = _seg64_sum(d * d, bd_ones) * (1.0 / RWKV_HEAD)
    yn = d * lax.rsqrt(var + RWKV_GN_EPS) * lnw_ref[...] + lnb_ref[...]
    bonus = _seg64_sum(r * kv * rk_ref[...], bd_ones) * v
    o_ref[...] = ((yn + bonus) * (g * _sigmoid(g))).astype(o_ref.dtype)


def _rwkv_branch(p, mu, w2a_bf, w0, a0, k_k, k_a, r_k, lnx_w, lnx_b, bsz, s):
    w = BRANCH_W
    L = RWKV_CHUNK
    nb = s // L
    row = lambda x: x.reshape(1, -1)
    mu4 = mu[:4 * w].reshape(4, w)
    mulo = mu[4 * w:].reshape(1, 2 * LORA)
    tok = lambda c: pl.BlockSpec((L, w), lambda b, i, c=c: (b * nb + i, P_OFF_B // w + c))
    par = lambda shp: pl.BlockSpec(shp, lambda b, i: (0,) * len(shp))
    return pl.pallas_call(
        _rwkv_kernel,
        out_shape=jax.ShapeDtypeStruct((bsz * s, w), BF16),
        grid=(bsz, nb),
        in_specs=[tok(0), tok(1), tok(2), tok(3),
                  pl.BlockSpec((L, 2 * LORA), lambda b, i: (b * nb + i, (P_OFF_B + 4 * w) // (2 * LORA))),
                  par((4, w)), par((1, 2 * LORA)), par((2 * LORA, 2 * w)),
                  par((1, w)), par((1, w)), par((1, w)), par((1, w)), par((1, w)),
                  par((1, w)), par((1, w))],
        out_specs=pl.BlockSpec((L, w), lambda b, i: (b * nb + i, 0)),
        scratch_shapes=[pltpu.VMEM((4, w), F32), pltpu.VMEM((1, 2 * LORA), F32),
                        pltpu.VMEM((w // LANES, LANES, LANES), F32)],
        compiler_params=_cparams(2),
        name="rwkv7",
    )(p, p, p, p, p, mu4, mulo, w2a_bf, row(w0), row(a0), row(k_k), row(k_a),
      row(r_k), row(lnx_w), row(lnx_b))


def _ret_kernel(q_ref, k_ref, v_ref, g_ref, cos_ref, sin_ref, o_ref, state_ref):
    C = q_ref.shape[0]
    dh = RET_HEAD
    half = dh // 2
    i = pl.program_id(1)

    @pl.when(i == 0)
    def _():
        state_ref[...] = jnp.zeros_like(state_ref)

    cos = cos_ref[...]
    sin = sin_ref[...]
    ri = lax.broadcasted_iota(jnp.int32, (C, C), 0)
    ci = lax.broadcasted_iota(jnp.int32, (C, C), 1)
    rel = (ri - ci).astype(F32)
    rowf = lax.broadcasted_iota(jnp.int32, (C, dh), 0).astype(F32)

    def rot(t):
        t1, t2 = t[:, :half], t[:, half:]
        return jnp.concatenate([t1 * cos - t2 * sin, t1 * sin + t2 * cos], axis=1)

    outs = []
    for h in range(RET_HEADS):
        sl = slice(h * dh, (h + 1) * dh)
        log_g = math.log1p(-(2.0 ** (-5.0 - h)))
        q = rot(q_ref[:, sl].astype(F32))
        k = rot(k_ref[:, sl].astype(F32)) * (dh ** -0.5)
        v_bf = v_ref[:, sl]
        mask = jnp.where(rel >= 0.0, jnp.exp(log_g * jnp.maximum(rel, 0.0)), 0.0)
        scores = _dot_nt(q.astype(BF16), k.astype(BF16)) * mask
        inner = _dot(scores.astype(BF16), v_bf)
        q_dec = jnp.exp(log_g * (rowf + 1.0))
        k_dec = jnp.exp(log_g * (C - 1.0 - rowf))
        st = state_ref[h]
        cross = _dot((q * q_dec).astype(BF16), st.astype(BF16))
        state_ref[h] = st * math.exp(log_g * C) + _dot((k * k_dec).T.astype(BF16), v_bf)
        out = inner + cross
        mu = jnp.mean(out, axis=-1, keepdims=True)
        dlt = out - mu
        var = jnp.mean(dlt * dlt, axis=-1, keepdims=True)
        outs.append(dlt * lax.rsqrt(var + RET_GN_EPS))
    out = jnp.concatenate(outs, axis=1)
    g = g_ref[...].astype(F32)
    o_ref[...] = (out * (g * _sigmoid(g))).astype(o_ref.dtype)


def _ret_branch(p, cos_t, sin_t, bsz, s):
    w = BRANCH_W
    C = RET_CHUNK
    nb = s // C
    tok = lambda c: pl.BlockSpec((C, w), lambda b, i, c=c: (b * nb + i, P_OFF_C // w + c))
    tab = pl.BlockSpec((C, RET_HEAD // 2), lambda b, i: (i, 0))
    return pl.pallas_call(
        _ret_kernel,
        out_shape=jax.ShapeDtypeStruct((bsz * s, w), BF16),
        grid=(bsz, nb),
        in_specs=[tok(0), tok(1), tok(2), tok(3), tab, tab],
        out_specs=pl.BlockSpec((C, w), lambda b, i: (b * nb + i, 0)),
        scratch_shapes=[pltpu.VMEM((RET_HEADS, RET_HEAD, RET_HEAD), F32)],
        compiler_params=_cparams(2),
        name="retention",
    )(p, p, p, p, cos_t, sin_t)


def _merge_kernel(x_ref, ya_ref, yb_ref, yc_ref, pm_ref, bm_ref, wb_ref, wo_ref, fw_ref, o_ref,
                  *, final_norm):
    d = x_ref.shape[1]
    merged = None
    for n, y_ref in enumerate((ya_ref, yb_ref, yc_ref)):
        proj = _dot(y_ref[...], wb_ref[n])
        gate = _sigmoid(pm_ref[:, n * d:(n + 1) * d].astype(F32) + bm_ref[:, n * d:(n + 1) * d])
        merged = gate * proj if merged is None else merged + gate * proj
    out = x_ref[...] + _dot(merged.astype(BF16), wo_ref[...])
    if final_norm:
        ms = jnp.mean(out * out, axis=-1, keepdims=True)
        out = out * lax.rsqrt(ms + EPS) * fw_ref[...]
    o_ref[...] = out


def _merge(x2, y_a, y_b, y_c, p, b_merge, w_branch_bf, w_out_bf, final_w, final_norm, tm):
    t, d = x2.shape
    w = BRANCH_W
    rows = lambda n: pl.BlockSpec((tm, n), lambda i: (i, 0))
    return pl.pallas_call(
        functools.partial(_merge_kernel, final_norm=final_norm),
        out_shape=jax.ShapeDtypeStruct((t, d), F32),
        grid=(t // tm,),
        in_specs=[rows(d), rows(w), rows(w), rows(w),
                  pl.BlockSpec((tm, 3 * d), lambda i: (i, P_OFF_M // (3 * d))),
                  pl.BlockSpec((1, 3 * d), lambda i: (0, 0)),
                  pl.BlockSpec((3, w, d), lambda i: (0, 0, 0)),
                  pl.BlockSpec((d, d), lambda i: (0, 0)),
                  pl.BlockSpec((1, d), lambda i: (0, 0))],
        out_specs=rows(d),
        compiler_params=_cparams(1),
        name="merge",
    )(x2, y_a, y_b, y_c, p, b_merge.reshape(1, 3 * d), w_branch_bf, w_out_bf,
      final_w.reshape(1, d))


def _rope_tables(s):
    half = RET_HEAD // 2
    inv_freq = 1.0 / (ROPE_BASE ** jnp.linspace(0.0, 1.0, half, dtype=F32))
    ang = jnp.arange(s)[:, None].astype(F32) * inv_freq
    return jnp.cos(ang), jnp.sin(ang)


def _lora_block(decay_w2, iclr_a2):
    z = jnp.zeros((LORA, BRANCH_W), F32)
    return jnp.concatenate([jnp.concatenate([decay_w2, z], axis=1),
                            jnp.concatenate([z, iclr_a2], axis=1)], axis=0).astype(BF16)


def kernel(x, norm_w, w_in, b_merge, conv_w, conv_b, lru_gate_w, lru_gate_b, lru_lambda, shift_mu,
           decay_w0, decay_w2, iclr_a0, iclr_a2, k_k, k_a, r_k, lnx_w, lnx_b, w_branch, w_out,
           final_norm_w):
    bsz, s, d = x.shape
    depth = norm_w.shape[0]
    t = bsz * s
    tm = min(512, t)
    tm_proj = min(1024, t)
    cos_t, sin_t = _rope_tables(s)
    x2 = x.reshape(t, d)
    for l in range(depth):
        w = w_in[l]
        w_bf = jnp.concatenate([w[:, OFF_C:OFF_M], w[:, :OFF_B], w[:, OFF_M:], w[:, OFF_B:OFF_C]],
                               axis=1).astype(BF16)
        p = _norm_matmul(x2, norm_w[l], w_bf, tm_proj, PROJ_TN)
        y_a = _lru_branch(p, conv_w[l], conv_b[l], lru_gate_w[l].astype(BF16), lru_gate_b[l],
                          lru_lambda[l], bsz, s)
        y_b = _rwkv_branch(p, shift_mu[l], _lora_block(decay_w2[l], iclr_a2[l]), decay_w0[l],
                           iclr_a0[l], k_k[l], k_a[l], r_k[l].reshape(-1), lnx_w[l], lnx_b[l],
                           bsz, s)
        y_c = _ret_branch(p, cos_t, sin_t, bsz, s)
        x2 = _merge(x2, y_a, y_b, y_c, p, b_merge[l], w_branch[l].astype(BF16),
                    w_out[l].astype(BF16), final_norm_w, l == depth - 1, tm)
    return x2.reshape(bsz, s, d)
```

```python
import functools
import math

import jax
import jax.numpy as jnp
from jax import lax
from jax.experimental import pallas as pl
from jax.experimental.pallas import tpu as pltpu

F32 = jnp.float32
BF16 = jnp.bfloat16

D_MODEL = 1024
EPS = 1e-6
BRANCH_W = 1024
LRU_BLOCKS = 4
LRU_BLOCK = BRANCH_W // LRU_BLOCKS
CONV_W = 4
LRU_C = 8.0
RWKV_HEAD = 64
LORA = 64
RWKV_GN_EPS = 64e-5
RET_HEADS = 4
RET_HEAD = BRANCH_W // RET_HEADS
RET_CHUNK = 128
ROPE_BASE = 10000.0
RET_GN_EPS = 1e-5
B_COLS = 4 * BRANCH_W + 2 * LORA
OFF_B = 2 * BRANCH_W
OFF_C = OFF_B + B_COLS
OFF_M = OFF_C + 4 * BRANCH_W
N_IN = OFF_M + 3 * D_MODEL
P_OFF_C = 0
P_OFF_A = 4 * BRANCH_W
P_OFF_M = P_OFF_A + 2 * BRANCH_W
P_OFF_B = P_OFF_M + 3 * D_MODEL

LANES = 128
SUBLANES = 8
MXU_DIM = 256
RWKV_CHUNK = 64
RWKV_TB = 128
LRU_TB = 256
PROJ_TN = 1920
VMEM_LIMIT = 48 * 1024 * 1024


def _cparams(n_axes):
    return pltpu.CompilerParams(dimension_semantics=("arbitrary",) * n_axes,
                                vmem_limit_bytes=VMEM_LIMIT)


def _sigmoid(x):
    return 1.0 / (1.0 + jnp.exp(-x))


def _softplus(x):
    return jnp.maximum(x, 0.0) + jnp.log(1.0 + jnp.exp(-jnp.abs(x)))


def _dot(a, b):
    return jnp.dot(a, b, preferred_element_type=F32)


def _dot_nt(a, b):
    return lax.dot_general(a, b, (((1,), (1,)), ((), ())), preferred_element_type=F32)


def _bf(x):
    return x.astype(BF16)


def _split_hi_lo(x):
    hi = x.astype(BF16)
    lo = (x - hi.astype(F32)).astype(BF16)
    return hi, lo


def _norm_matmul_kernel(x_ref, nw_ref, w_ref, o_ref, h_ref):
    @pl.when(pl.program_id(1) == 0)
    def _():
        x = x_ref[...]
        ms = jnp.mean(x * x, axis=-1, keepdims=True)
        h_ref[...] = (x * lax.rsqrt(ms + EPS) * nw_ref[...]).astype(BF16)

    o_ref[...] = _dot(h_ref[...], w_ref[...]).astype(o_ref.dtype)


def _norm_matmul(x2, norm_w, w_bf, tm, tn):
    t, d = x2.shape
    n = w_bf.shape[1]
    return pl.pallas_call(
        _norm_matmul_kernel,
        out_shape=jax.ShapeDtypeStruct((t, n), BF16),
        grid=(t // tm, n // tn),
        in_specs=[pl.BlockSpec((tm, d), lambda i, j: (i, 0)),
                  pl.BlockSpec((1, d), lambda i, j: (0, 0)),
                  pl.BlockSpec((d, tn), lambda i, j: (0, j))],
        out_specs=pl.BlockSpec((tm, tn), lambda i, j: (i, j)),
        scratch_shapes=[pltpu.VMEM((tm, d), BF16)],
        compiler_params=_cparams(2),
        name="norm_matmul",
    )(x2, norm_w.reshape(1, d), w_bf)


def _lru_kernel(xa_ref, ga_ref, cw_ref, cb_ref, gw_ref, gb_ref, lam_ref, o_ref,
                tail_ref, a_ref, b_ref, hs_ref, h_ref):
    tb = xa_ref.shape[0]
    w = xa_ref.shape[1]
    i = pl.program_id(1)

    @pl.when(i == 0)
    def _():
        tail_ref[...] = jnp.zeros_like(tail_ref)
        h_ref[...] = jnp.zeros_like(h_ref)

    x = xa_ref[...].astype(F32)
    xfull = jnp.concatenate([tail_ref[...], x], axis=0)
    tail_ref[...] = x[tb - SUBLANES:, :]
    conv = cb_ref[...] + x * cw_ref[CONV_W - 1:CONV_W, :]
    for s in range(1, CONV_W):
        shifted = pltpu.roll(xfull, s, 0)[SUBLANES:, :]
        conv = conv + shifted * cw_ref[CONV_W - 1 - s:CONV_W - s, :]

    conv_bf = conv.astype(BF16)
    g0, g1 = [], []
    for n in range(LRU_BLOCKS):
        cn = conv_bf[:, n * LRU_BLOCK:(n + 1) * LRU_BLOCK]
        g0.append(_dot(cn, gw_ref[0, n]))
        g1.append(_dot(cn, gw_ref[1, n]))
    g0 = jnp.concatenate(g0, axis=1) + gb_ref[0:1, :]
    g1 = jnp.concatenate(g1, axis=1) + gb_ref[1:2, :]
    r = _sigmoid(g0)
    ig = _sigmoid(g1)
    log_a = (-LRU_C) * r * _softplus(-lam_ref[...])
    a = jnp.exp(log_a)
    z = -jnp.tanh(log_a) * (a * a + 1.0)
    mult = jnp.where(z > 0.0, z * lax.rsqrt(z), 0.0)
    row = lax.broadcasted_iota(jnp.int32, (tb, w), 0)
    mult = jnp.where((row == 0) & (i == 0), 1.0, mult)
    a_ref[...] = a
    b_ref[...] = mult * ig * conv

    sub = lax.broadcasted_iota(jnp.int32, (SUBLANES, w), 0)

    def body(g, h):
        base = pl.multiple_of(g * SUBLANES, SUBLANES)
        a8 = a_ref[pl.ds(base, SUBLANES), :]
        b8 = b_ref[pl.ds(base, SUBLANES), :]
        d = 1
        while d < SUBLANES:
            keep = sub >= d
            a_prev = jnp.where(keep, pltpu.roll(a8, d, 0), 1.0)
            b_prev = jnp.where(keep, pltpu.roll(b8, d, 0), 0.0)
            b8 = a8 * b_prev + b8
            a8 = a8 * a_prev
            d *= 2
        hs = a8 * h + b8
        hs_ref[pl.ds(base, SUBLANES), :] = hs
        return hs[SUBLANES - 1:SUBLANES, :]

    h_ref[...] = lax.fori_loop(0, tb // SUBLANES, body, h_ref[...], unroll=4)
    ga = ga_ref[...].astype(F32)
    o_ref[...] = (hs_ref[...] * (ga * _sigmoid(ga))).astype(o_ref.dtype)


def _lru_branch(p, conv_w, conv_b, gate_w_bf, gate_b, lam, bsz, s):
    w = BRANCH_W
    tb = min(LRU_TB, s)
    nb = s // tb
    return pl.pallas_call(
        _lru_kernel,
        out_shape=jax.ShapeDtypeStruct((bsz * s, w), BF16),
        grid=(bsz, nb),
        in_specs=[pl.BlockSpec((tb, w), lambda b, i: (b * nb + i, P_OFF_A // w)),
                  pl.BlockSpec((tb, w), lambda b, i: (b * nb + i, P_OFF_A // w + 1)),
                  pl.BlockSpec((CONV_W, w), lambda b, i: (0, 0)),
                  pl.BlockSpec((1, w), lambda b, i: (0, 0)),
                  pl.BlockSpec((2, LRU_BLOCKS, LRU_BLOCK, LRU_BLOCK), lambda b, i: (0, 0, 0, 0)),
                  pl.BlockSpec((2, w), lambda b, i: (0, 0)),
                  pl.BlockSpec((1, w), lambda b, i: (0, 0))],
        out_specs=pl.BlockSpec((tb, w), lambda b, i: (b * nb + i, 0)),
        scratch_shapes=[pltpu.VMEM((SUBLANES, w), F32), pltpu.VMEM((tb, w), F32),
                        pltpu.VMEM((tb, w), F32), pltpu.VMEM((tb, w), F32),
                        pltpu.VMEM((1, w), F32)],
        compiler_params=_cparams(2),
        name="rglru",
    )(p, p, conv_w, conv_b.reshape(1, w), gate_w_bf, gate_b, lam.reshape(1, w))


def _seg64_sum(x, bd_ones):
    xb = _bf(x)
    out = [_dot(xb[:, j * MXU_DIM:(j + 1) * MXU_DIM], bd_ones) for j in range(x.shape[1] // MXU_DIM)]
    return jnp.concatenate(out, axis=1)


def _shift_rows(x, prev_row):
    rolled = pltpu.roll(x, 1, 0)
    head = rolled[0:SUBLANES, :]
    row = lax.broadcasted_iota(jnp.int32, head.shape, 0)
    head = jnp.where(row == 0, jnp.broadcast_to(prev_row, head.shape), head)
    return jnp.concatenate([head, rolled[SUBLANES:, :]], axis=0)


def _rwkv_kernel(r_ref, k_ref, v_ref, g_ref, lo_ref, mu_ref, mulo_ref, w2a_ref, w0_ref, a0_ref,
                 kk_ref, ka_ref, rk_ref, lnw_ref, lnb_ref, o_ref, prev_ref, prevlo_ref, state_ref):
    nbat, tb, w = r_ref.shape
    L = RWKV_CHUNK
    G = MXU_DIM
    n_chunks = tb // L
    n_groups = w // G
    hshift = RWKV_HEAD.bit_length() - 1

    @pl.when(pl.program_id(0) == 0)
    def _():
        prev_ref[...] = jnp.zeros_like(prev_ref)
        prevlo_ref[...] = jnp.zeros_like(prevlo_ref)
        state_ref[...] = jnp.zeros_like(state_ref)

    rg = lax.broadcasted_iota(jnp.int32, (G, G), 0)
    cg = lax.broadcasted_iota(jnp.int32, (G, G), 1)
    same_head = (rg >> hshift) == (cg >> hshift)
    bd_ones = same_head.astype(BF16)
    r2 = lax.broadcasted_iota(jnp.int32, (2 * L, G), 0)
    c2 = lax.broadcasted_iota(jnp.int32, (2 * L, G), 1) & (L - 1)
    score_mask = (c2 < jnp.where(r2 >= L, r2 - (L - 1), r2)).astype(BF16)
    re = lax.broadcasted_iota(jnp.int32, (L, G), 0)
    ce = lax.broadcasted_iota(jnp.int32, (L, G), 1) & (L - 1)
    strict_w = ce < re
    eye_w = (ce == re).astype(F32)
    rt = lax.broadcasted_iota(jnp.int32, (tb, tb), 0)
    ct = lax.broadcasted_iota(jnp.int32, (tb, tb), 1)
    tril_chunks = ((ct <= rt) & ((ct >> hshift) == (rt >> hshift))).astype(BF16)
    assert L == RWKV_HEAD
    heads_per_tile = LANES // RWKV_HEAD
    lane_t = lax.broadcasted_iota(jnp.int32, (L, LANES), 1) >> hshift
    head_pick = [(lane_t == hh).astype(BF16) for hh in range(heads_per_tile)]
    tile_bd = bd_ones[0:LANES, 0:LANES]
    zero_lt = jnp.zeros((L, LANES), BF16)

    def bd(x):
        xb = _bf(x)
        cols = []
        for lt in range(G // LANES):
            tile = xb[:, lt * LANES:(lt + 1) * LANES]
            blocks = [zero_lt] * (G // L)
            for hh in range(heads_per_tile):
                blocks[lt * heads_per_tile + hh] = tile * head_pick[hh]
            cols.append(jnp.concatenate(blocks, axis=0))
        return jnp.concatenate(cols, axis=1)

    def token_shift(x, prev_row, mu_row):
        return x + (_shift_rows(x, prev_row) - x) * mu_row

    units = []
    post = []
    for bi in range(nbat):
        r_raw, k_raw, v_raw, g_raw, lo_raw = (ref[bi].astype(F32)
                                              for ref in (r_ref, k_ref, v_ref, g_ref, lo_ref))
        r = token_shift(r_raw, prev_ref[bi, 0:1, :], mu_ref[0:1, :])
        k = token_shift(k_raw, prev_ref[bi, 1:2, :], mu_ref[1:2, :])
        v = token_shift(v_raw, prev_ref[bi, 2:3, :], mu_ref[2:3, :])
        g = token_shift(g_raw, prev_ref[bi, 3:4, :], mu_ref[3:4, :])
        lo = token_shift(lo_raw, prevlo_ref[bi], mulo_ref[...])
        prev_ref[bi, 0:1, :] = r_raw[tb - 1:tb, :]
        prev_ref[bi, 1:2, :] = k_raw[tb - 1:tb, :]
        prev_ref[bi, 2:3, :] = v_raw[tb - 1:tb, :]
        prev_ref[bi, 3:4, :] = g_raw[tb - 1:tb, :]
        prevlo_ref[bi] = lo_raw[tb - 1:tb, :]

        lane_lo = lax.broadcasted_iota(jnp.int32, lo.shape, 1)
        lo_in = _bf(jnp.where(lane_lo < LORA, jnp.tanh(lo), lo))
        dwa = _dot(lo_in, w2a_ref[...])
        ld = (-math.exp(-0.5)) * _sigmoid(w0_ref[...] + dwa[:, :w])
        a = _sigmoid(a0_ref[...] + dwa[:, w:])

        kk = k * kk_ref[...]
        kk = kk * lax.rsqrt(jnp.maximum(_seg64_sum(kk * kk, bd_ones), 1e-24))
        kv = k * (1.0 + (a - 1.0) * ka_ref[...])
        b = kk * a

        ld_hi, ld_lo = _split_hi_lo(ld)
        cum = _dot(tril_chunks, ld_hi) + _dot(tril_chunks, ld_lo)
        p_last = [jnp.exp(cum[(c + 1) * L - 1:(c + 1) * L, :]) for c in range(n_chunks)]
        p_inc = jnp.exp(cum)
        p_exc = jnp.exp(cum - ld)
        p_inv = jnp.exp(-cum)
        p_hat = p_inv * jnp.concatenate([jnp.broadcast_to(pl_, (L, w)) for pl_ in p_last], axis=0)

        kap_t = kk * p_exc
        r_t = r * p_inc
        b_t = b * p_inv
        k_t = kv * p_inv
        b_h = b * p_hat
        k_h = kv * p_hat
        post.append((r, kv, v, g))
        for c in range(n_chunks):
            rows = slice(c * L, (c + 1) * L)
            for gi in range(n_groups):
                cols = slice(gi * G, (gi + 1) * G)
                units.append(dict(
                    c=c, bi=bi, gi=gi,
                    gq=_bf(jnp.concatenate([kap_t[rows, cols], r_t[rows, cols]], axis=0)),
                    b_t=b_t[rows, cols], k_t=k_t[rows, cols], v=v[rows, cols],
                    zh=jnp.concatenate([b_h[rows, cols], k_h[rows, cols]], axis=0),
                    p_last=p_last[c][:, cols]))

    for u in units:
        scb = _dot_nt(u["gq"], bd(u["b_t"]))
        sck = _dot_nt(u["gq"], bd(u["k_t"]))
        u["pw"] = jnp.where(strict_w, -scb[0:L], 0.0)
        u["r_b"] = _bf(scb[L:2 * L]) * score_mask[L:2 * L]
        u["ak_rk"] = _bf(sck) * score_mask
        u["tinv"] = eye_w + u["pw"]
    for u in units:
        akrk = _dot(u["ak_rk"], bd(u["v"]))
        u["akv"] = akrk[0:L]
        u["rkv"] = akrk[L:2 * L]
        u["zh_t"] = _bf(u["zh"].T)
        u["p_col"] = jnp.broadcast_to(u["p_last"], (LANES, G)).T
    for u in units:
        u["pw_next"] = _dot(_bf(u["pw"]), bd(u["pw"]))
    for _ in range(int(math.log2(L)) - 2):
        for u in units:
            u["pw"] = u["pw_next"]
            both = _dot(_bf(jnp.concatenate([u["tinv"], u["pw"]], axis=0)), bd(u["pw"]))
            u["tinv"] = u["tinv"] + both[0:L]
            u["pw_next"] = both[L:2 * L]
    for u in units:
        u["tinv"] = u["tinv"] + _dot(_bf(u["tinv"]), bd(u["pw_next"]))

    ys = {}
    n_q = G // LANES
    zero_q = jnp.zeros((LANES, LANES), BF16)
    for c in range(n_chunks):
        cu = [u for u in units if u["c"] == c]
        for u in cu:
            st = [state_ref[u["bi"], u["gi"], q] for q in range(n_q)]
            u["st"] = st
            st_bd = jnp.concatenate(
                [jnp.concatenate([_bf(st[q]) * tile_bd if qq == q else zero_q for qq in range(n_q)],
                                 axis=1) for q in range(n_q)], axis=0)
            x0 = _dot(u["gq"], st_bd)
            u["rhs"] = -(x0[0:L] + u["akv"])
            u["y"] = x0[L:2 * L] + u["rkv"]
        for u in cu:
            u["u"] = _dot(_bf(u["tinv"]), bd(u["rhs"]))
        for u in cu:
            ys[(u["bi"], c, u["gi"])] = u["y"] + _dot(u["r_b"], bd(u["u"]))
            uv = _bf(jnp.concatenate([u["u"], u["v"]], axis=0))
            for q in range(n_q):
                qs = slice(q * LANES, (q + 1) * LANES)
                state_ref[u["bi"], u["gi"], q] = (u["st"][q] * u["p_col"][qs, :]
                                                  + _dot(u["zh_t"][qs, :], uv[:, qs]))

    for bi in range(nbat):
        r, kv, v, g = post[bi]
        y = jnp.concatenate(
            [jnp.concatenate([ys[(bi, c, gi)] for gi in range(n_groups)], axis=1)
             for c in range(n_chunks)], axis=0)
        mean = _seg64_sum(y, bd_ones) * (1.0 / RWKV_HEAD)
        d = y - mean
        var = _seg64_sum(d * d, bd_ones) * (1.0 / RWKV_HEAD)
        yn = d * lax.rsqrt(var + RWKV_GN_EPS) * lnw_ref[...] + lnb_ref[...]
        bonus = _seg64_sum(r * kv * rk_ref[...], bd_ones) * v
        o_ref[bi] = ((yn + bonus) * (g * _sigmoid(g))).astype(o_ref.dtype)


def _rwkv_branch(p, mu, w2a_bf, w0, a0, k_k, k_a, r_k, lnx_w, lnx_b, bsz, s):
    w = BRANCH_W
    tb = min(RWKV_TB, s)
    nb = s // tb
    p3 = p.reshape(bsz, s, p.shape[1])
    row = lambda x: x.reshape(1, -1)
    mu4 = mu[:4 * w].reshape(4, w)
    mulo = mu[4 * w:].reshape(1, 2 * LORA)
    tok = lambda c: pl.BlockSpec((bsz, tb, w), lambda i, c=c: (0, i, P_OFF_B // w + c))
    par = lambda shp: pl.BlockSpec(shp, lambda i: (0,) * len(shp))
    out = pl.pallas_call(
        _rwkv_kernel,
        out_shape=jax.ShapeDtypeStruct((bsz, s, w), BF16),
        grid=(nb,),
        in_specs=[tok(0), tok(1), tok(2), tok(3),
                  pl.BlockSpec((bsz, tb, 2 * LORA), lambda i: (0, i, (P_OFF_B + 4 * w) // (2 * LORA))),
                  par((4, w)), par((1, 2 * LORA)), par((2 * LORA, 2 * w)),
                  par((1, w)), par((1, w)), par((1, w)), par((1, w)), par((1, w)),
                  par((1, w)), par((1, w))],
        out_specs=pl.BlockSpec((bsz, tb, w), lambda i: (0, i, 0)),
        scratch_shapes=[pltpu.VMEM((bsz, 4, w), F32), pltpu.VMEM((bsz, 1, 2 * LORA), F32),
                        pltpu.VMEM((bsz, w // MXU_DIM, MXU_DIM // LANES, LANES, LANES), F32)],
        compiler_params=_cparams(1),
        name="rwkv7",
    )(p3, p3, p3, p3, p3, mu4, mulo, w2a_bf, row(w0), row(a0), row(k_k), row(k_a),
      row(r_k), row(lnx_w), row(lnx_b))
    return out.reshape(bsz * s, w)


def _ret_kernel(q_ref, k_ref, v_ref, g_ref, cos_ref, sin_ref, o_ref, state_ref):
    C = q_ref.shape[0]
    dh = RET_HEAD
    half = dh // 2
    i = pl.program_id(1)

    @pl.when(i == 0)
    def _():
        state_ref[...] = jnp.zeros_like(state_ref)

    cos = cos_ref[...]
    sin = sin_ref[...]
    ri = lax.broadcasted_iota(jnp.int32, (C, C), 0)
    ci = lax.broadcasted_iota(jnp.int32, (C, C), 1)
    rel = (ri - ci).astype(F32)
    rowf = lax.broadcasted_iota(jnp.int32, (C, dh), 0).astype(F32)

    def rot(t):
        t1, t2 = t[:, :half], t[:, half:]
        return jnp.concatenate([t1 * cos - t2 * sin, t1 * sin + t2 * cos], axis=1)

    outs = []
    for h in range(RET_HEADS):
        sl = slice(h * dh, (h + 1) * dh)
        log_g = math.log1p(-(2.0 ** (-5.0 - h)))
        q = rot(q_ref[:, sl].astype(F32))
        k = rot(k_ref[:, sl].astype(F32)) * (dh ** -0.5)
        v_bf = v_ref[:, sl]
        mask = jnp.where(rel >= 0.0, jnp.exp(log_g * jnp.maximum(rel, 0.0)), 0.0)
        scores = _dot_nt(q.astype(BF16), k.astype(BF16)) * mask
        inner = _dot(scores.astype(BF16), v_bf)
        q_dec = jnp.exp(log_g * (rowf + 1.0))
        k_dec = jnp.exp(log_g * (C - 1.0 - rowf))
        st = state_ref[h]
        cross = _dot((q * q_dec).astype(BF16), st.astype(BF16))
        state_ref[h] = st * math.exp(log_g * C) + _dot((k * k_dec).T.astype(BF16), v_bf)
        out = inner + cross
        mu = jnp.mean(out, axis=-1, keepdims=True)
        dlt = out - mu
        var = jnp.mean(dlt * dlt, axis=-1, keepdims=True)
        outs.append(dlt * lax.rsqrt(var + RET_GN_EPS))
    out = jnp.concatenate(outs, axis=1)
    g = g_ref[...].astype(F32)
    o_ref[...] = (out * (g * _sigmoid(g))).astype(o_ref.dtype)


def _ret_branch(p, cos_t, sin_t, bsz, s):
    w = BRANCH_W
    C = RET_CHUNK
    nb = s // C
    tok = lambda c: pl.BlockSpec((C, w), lambda b, i, c=c: (b * nb + i, P_OFF_C // w + c))
    tab = pl.BlockSpec((C, RET_HEAD // 2), lambda b, i: (i, 0))
    return pl.pallas_call(
        _ret_kernel,
        out_shape=jax.ShapeDtypeStruct((bsz * s, w), BF16),
        grid=(bsz, nb),
        in_specs=[tok(0), tok(1), tok(2), tok(3), tab, tab],
        out_specs=pl.BlockSpec((C, w), lambda b, i: (b * nb + i, 0)),
        scratch_shapes=[pltpu.VMEM((RET_HEADS, RET_HEAD, RET_HEAD), F32)],
        compiler_params=_cparams(2),
        name="retention",
    )(p, p, p, p, cos_t, sin_t)


def _merge_kernel(x_ref, ya_ref, yb_ref, yc_ref, pm_ref, bm_ref, wb_ref, wo_ref, fw_ref, o_ref,
                  *, final_norm):
    d = x_ref.shape[1]
    merged = None
    for n, y_ref in enumerate((ya_ref, yb_ref, yc_ref)):
        proj = _dot(y_ref[...], wb_ref[n])
        gate = _sigmoid(pm_ref[:, n * d:(n + 1) * d].astype(F32) + bm_ref[:, n * d:(n + 1) * d])
        merged = gate * proj if merged is None else merged + gate * proj
    out = x_ref[...] + _dot(merged.astype(BF16), wo_ref[...])
    if final_norm:
        ms = jnp.mean(out * out, axis=-1, keepdims=True)
        out = out * lax.rsqrt(ms + EPS) * fw_ref[...]
    o_ref[...] = out


def _merge(x2, y_a, y_b, y_c, p, b_merge, w_branch_bf, w_out_bf, final_w, final_norm, tm):
    t, d = x2.shape
    w = BRANCH_W
    rows = lambda n: pl.BlockSpec((tm, n), lambda i: (i, 0))
    return pl.pallas_call(
        functools.partial(_merge_kernel, final_norm=final_norm),
        out_shape=jax.ShapeDtypeStruct((t, d), F32),
        grid=(t // tm,),
        in_specs=[rows(d), rows(w), rows(w), rows(w),
                  pl.BlockSpec((tm, 3 * d), lambda i: (i, P_OFF_M // (3 * d))),
                  pl.BlockSpec((1, 3 * d), lambda i: (0, 0)),
                  pl.BlockSpec((3, w, d), lambda i: (0, 0, 0)),
                  pl.BlockSpec((d, d), lambda i: (0, 0)),
                  pl.BlockSpec((1, d), lambda i: (0, 0))],
        out_specs=rows(d),
        compiler_params=_cparams(1),
        name="merge",
    )(x2, y_a, y_b, y_c, p, b_merge.reshape(1, 3 * d), w_branch_bf, w_out_bf,
      final_w.reshape(1, d))


def _rope_tables(s):
    half = RET_HEAD // 2
    inv_freq = 1.0 / (ROPE_BASE ** jnp.linspace(0.0, 1.0, half, dtype=F32))
    ang = jnp.arange(s)[:, None].astype(F32) * inv_freq
    return jnp.cos(ang), jnp.sin(ang)


def _lora_block(decay_w2, iclr_a2):
    z = jnp.zeros((LORA, BRANCH_W), F32)
    return jnp.concatenate([jnp.concatenate([decay_w2, z], axis=1),
                            jnp.concatenate([z, iclr_a2], axis=1)], axis=0).astype(BF16)


def kernel(x, norm_w, w_in, b_merge, conv_w, conv_b, lru_gate_w, lru_gate_b, lru_lambda, shift_mu,
           decay_w0, decay_w2, iclr_a0, iclr_a2, k_k, k_a, r_k, lnx_w, lnx_b, w_branch, w_out,
           final_norm_w):
    bsz, s, d = x.shape
    depth = norm_w.shape[0]
    t = bsz * s
    tm = min(512, t)
    tm_proj = min(1024, t)
    cos_t, sin_t = _rope_tables(s)
    x2 = x.reshape(t, d)
    for l in range(depth):
        w = w_in[l]
        w_bf = jnp.concatenate([w[:, OFF_C:OFF_M], w[:, :OFF_B], w[:, OFF_M:], w[:, OFF_B:OFF_C]],
                               axis=1).astype(BF16)
        p = _norm_matmul(x2, norm_w[l], w_bf, tm_proj, PROJ_TN)
        y_a = _lru_branch(p, conv_w[l], conv_b[l], lru_gate_w[l].astype(BF16), lru_gate_b[l],
                          lru_lambda[l], bsz, s)
        y_b = _rwkv_branch(p, shift_mu[l], _lora_block(decay_w2[l], iclr_a2[l]), decay_w0[l],
                           iclr_a0[l], k_k[l], k_a[l], r_k[l].reshape(-1), lnx_w[l], lnx_b[l],
                           bsz, s)
        y_c = _ret_branch(p, cos_t, sin_t, bsz, s)
        x2 = _merge(x2, y_a, y_b, y_c, p, b_merge[l], w_branch[l].astype(BF16),
                    w_out[l].astype(BF16), final_norm_w, l == depth - 1, tm)
    return x2.reshape(bsz, s, d)
```

```python
import functools
import math

import jax
import jax.numpy as jnp
from jax import lax
from jax.experimental import pallas as pl
from jax.experimental.pallas import tpu as pltpu

F32 = jnp.float32
BF16 = jnp.bfloat16

D_MODEL = 1024
EPS = 1e-6
BRANCH_W = 1024
LRU_BLOCKS = 4
LRU_BLOCK = BRANCH_W // LRU_BLOCKS
CONV_W = 4
LRU_C = 8.0
RWKV_HEAD = 64
LORA = 64
RWKV_GN_EPS = 64e-5
RET_HEADS = 4
RET_HEAD = BRANCH_W // RET_HEADS
RET_CHUNK = 128
ROPE_BASE = 10000.0
RET_GN_EPS = 1e-5
B_COLS = 4 * BRANCH_W + 2 * LORA
OFF_B = 2 * BRANCH_W
OFF_C = OFF_B + B_COLS
OFF_M = OFF_C + 4 * BRANCH_W
N_IN = OFF_M + 3 * D_MODEL
LANES = 128
SUBLANES = 8
MXU_DIM = 256
RWKV_CHUNK = 64
RWKV_TB = 128
PROJ_TM = 512
LRU_PIECE_ROWS = 64
P_OFF_M = 0
P_OFF_C = P_OFF_M + 3 * D_MODEL
P_OFF_B = P_OFF_C + 4 * BRANCH_W
PROJ_TN = -(-(P_OFF_B + B_COLS) // (LRU_BLOCKS * MXU_DIM)) * MXU_DIM
P_COLS = PROJ_TN * LRU_BLOCKS
P_PAD = P_COLS - (P_OFF_B + B_COLS)
VMEM_LIMIT = 48 * 1024 * 1024


def _cparams(n_axes):
    return pltpu.CompilerParams(dimension_semantics=("arbitrary",) * n_axes,
                                vmem_limit_bytes=VMEM_LIMIT)


def _sigmoid(x):
    return 1.0 / (1.0 + jnp.exp(-x))


def _softplus(x):
    return jnp.maximum(x, 0.0) + jnp.log(1.0 + jnp.exp(-jnp.abs(x)))


def _dot(a, b):
    return jnp.dot(a, b, preferred_element_type=F32)


def _dot_nt(a, b):
    return lax.dot_general(a, b, (((1,), (1,)), ((), ())), preferred_element_type=F32)


def _bf(x):
    return x.astype(BF16)


def _split_hi_lo(x):
    hi = x.astype(BF16)
    lo = (x - hi.astype(F32)).astype(BF16)
    return hi, lo


def _lru_pieces(xg_ref, n, cw, cb, gw, gb, lam, tail, h, first, rows_per_piece, carry_out):
    tm, w = xg_ref.shape[2], xg_ref.shape[3]
    tail = jnp.where(first, 0.0, tail)
    h = jnp.where(first, 0.0, h)
    decay_rate = _softplus(-lam)
    sub = lax.broadcasted_iota(jnp.int32, (SUBLANES, w), 0)
    row = lax.broadcasted_iota(jnp.int32, (rows_per_piece, w), 0)
    for piece in range(tm // rows_per_piece):
        rows = slice(piece * rows_per_piece, (piece + 1) * rows_per_piece)
        x = xg_ref[0, n, rows, :]
        ga = xg_ref[1, n, rows, :]
        xfull = jnp.concatenate([tail, x], axis=0)
        tail = x[rows_per_piece - SUBLANES:, :]
        conv = cb + x * cw[CONV_W - 1:CONV_W, :]
        for s in range(1, CONV_W):
            shifted = pltpu.roll(xfull, s, 0)[SUBLANES:, :]
            conv = conv + shifted * cw[CONV_W - 1 - s:CONV_W - s, :]
        conv_bf = _bf(conv)
        r = _sigmoid(_dot(conv_bf, gw[0]) + gb[0:1, :])
        ig = _sigmoid(_dot(conv_bf, gw[1]) + gb[1:2, :])
        log_a = (-LRU_C) * r * decay_rate
        a = jnp.exp(log_a)
        z = -jnp.tanh(log_a) * (a * a + 1.0)
        mult = jnp.where(z > 0.0, z * lax.rsqrt(z), 0.0)
        if piece == 0:
            mult = jnp.where((row == 0) & first, 1.0, mult)
        b = mult * ig * conv

        tiles = []
        for g in range(rows_per_piece // SUBLANES):
            a8 = a[g * SUBLANES:(g + 1) * SUBLANES, :]
            b8 = b[g * SUBLANES:(g + 1) * SUBLANES, :]
            d = 1
            while d < SUBLANES:
                keep = sub >= d
                a_prev = jnp.where(keep, pltpu.roll(a8, d, 0), 1.0)
                b_prev = jnp.where(keep, pltpu.roll(b8, d, 0), 0.0)
                b8 = a8 * b_prev + b8
                a8 = a8 * a_prev
                d *= 2
            hs = a8 * h + b8
            tiles.append(hs)
            h = hs[SUBLANES - 1:SUBLANES, :]
        carry_out[:] = [tail, h]
        yield rows, jnp.concatenate(tiles, axis=0) * (ga * _sigmoid(ga))


def _proj_lru_kernel(x_ref, nw_ref, w_ref, wl_ref, cw_ref, cb_ref, gw_ref, gb_ref, lam_ref,
                     p_ref, ya_ref, h_ref, xg_ref, tail_ref, hc_ref, *, blocks_per_seq):
    i = pl.program_id(0)
    n = pl.program_id(1)

    @pl.when(n == 0)
    def _():
        x = x_ref[...]
        ms = jnp.mean(x * x, axis=-1, keepdims=True)
        h = _bf(x * lax.rsqrt(ms + EPS) * nw_ref[...])
        h_ref[...] = h
        xg = _dot(h, wl_ref[...])
        for half in range(2):
            for blk in range(LRU_BLOCKS):
                c0 = half * BRANCH_W + blk * LRU_BLOCK
                xg_ref[half, blk] = xg[:, c0:c0 + LRU_BLOCK]

    first = (i % blocks_per_seq) == 0
    carry = []
    pieces = _lru_pieces(xg_ref, n, cw_ref[n], cb_ref[n], gw_ref[n], gb_ref[n], lam_ref[n],
                         tail_ref[n], hc_ref[n], first, LRU_PIECE_ROWS, carry)
    h = h_ref[...]
    for c in range(p_ref.shape[1] // MXU_DIM):
        cols = slice(c * MXU_DIM, (c + 1) * MXU_DIM)
        p_ref[:, cols] = _dot(h, w_ref[:, cols]).astype(p_ref.dtype)
        piece = next(pieces, None)
        if piece is not None:
            rows, y = piece
            ya_ref[n, rows, :] = y.astype(ya_ref.dtype)
    assert next(pieces, None) is None
    tail_ref[n], hc_ref[n] = carry


def _proj_lru(x2, norm_w, w_main_bf, w_lru_bf, conv_w, conv_b, gate_w_bf, gate_b, lam, s):
    t, d = x2.shape
    tm = min(PROJ_TM, s)
    nblk, blk = LRU_BLOCKS, LRU_BLOCK
    per_block = lambda a: a.reshape(a.shape[0], nblk, blk).transpose(1, 0, 2)
    par = lambda shp: pl.BlockSpec(shp, lambda i, j: (0,) * len(shp))
    return pl.pallas_call(
        functools.partial(_proj_lru_kernel, blocks_per_seq=s // tm),
        out_shape=(jax.ShapeDtypeStruct((t, P_COLS), BF16),
                   jax.ShapeDtypeStruct((nblk, t, blk), BF16)),
        grid=(t // tm, nblk),
        in_specs=[pl.BlockSpec((tm, d), lambda i, j: (i, 0)),
                  par((1, d)),
                  pl.BlockSpec((d, PROJ_TN), lambda i, j: (0, j)),
                  par((d, 2 * BRANCH_W)),
                  par((nblk, CONV_W, blk)), par((nblk, 1, blk)), par((nblk, 2, blk, blk)),
                  par((nblk, 2, blk)), par((nblk, 1, blk))],
        out_specs=(pl.BlockSpec((tm, PROJ_TN), lambda i, j: (i, j)),
                   pl.BlockSpec((nblk, tm, blk), lambda i, j: (0, i, 0))),
        scratch_shapes=[pltpu.VMEM((tm, d), BF16),
                        pltpu.VMEM((2, nblk, tm, blk), F32),
                        pltpu.VMEM((nblk, SUBLANES, blk), F32),
                        pltpu.VMEM((nblk, 1, blk), F32)],
        compiler_params=_cparams(2),
        name="proj_rglru",
    )(x2, norm_w.reshape(1, d), w_main_bf, w_lru_bf, per_block(conv_w),
      per_block(conv_b.reshape(1, -1)), gate_w_bf.transpose(1, 0, 2, 3), per_block(gate_b),
      per_block(lam.reshape(1, -1)))


def _seg64_sum(x, bd_ones):
    xb = _bf(x)
    out = [_dot(xb[:, j * MXU_DIM:(j + 1) * MXU_DIM], bd_ones) for j in range(x.shape[1] // MXU_DIM)]
    return jnp.concatenate(out, axis=1)


def _shift_rows(x, prev_row):
    rolled = pltpu.roll(x, 1, 0)
    head = rolled[0:SUBLANES, :]
    row = lax.broadcasted_iota(jnp.int32, head.shape, 0)
    head = jnp.where(row == 0, jnp.broadcast_to(prev_row, head.shape), head)
    return jnp.concatenate([head, rolled[SUBLANES:, :]], axis=0)


def _rwkv_kernel(r_ref, k_ref, v_ref, g_ref, lo_ref, mu_ref, mulo_ref, w2a_ref, w0_ref, a0_ref,
                 kk_ref, ka_ref, rk_ref, lnw_ref, lnb_ref, o_ref, prev_ref, prevlo_ref, state_ref):
    nbat, tb, w = r_ref.shape
    L = RWKV_CHUNK
    G = MXU_DIM
    n_chunks = tb // L
    n_groups = w // G
    hshift = RWKV_HEAD.bit_length() - 1

    @pl.when(pl.program_id(0) == 0)
    def _():
        prev_ref[...] = jnp.zeros_like(prev_ref)
        prevlo_ref[...] = jnp.zeros_like(prevlo_ref)
        state_ref[...] = jnp.zeros_like(state_ref)

    rg = lax.broadcasted_iota(jnp.int32, (G, G), 0)
    cg = lax.broadcasted_iota(jnp.int32, (G, G), 1)
    same_head = (rg >> hshift) == (cg >> hshift)
    bd_ones = same_head.astype(BF16)
    r2 = lax.broadcasted_iota(jnp.int32, (2 * L, G), 0)
    c2 = lax.broadcasted_iota(jnp.int32, (2 * L, G), 1) & (L - 1)
    score_mask = (c2 < jnp.where(r2 >= L, r2 - (L - 1), r2)).astype(BF16)
    re = lax.broadcasted_iota(jnp.int32, (L, G), 0)
    ce = lax.broadcasted_iota(jnp.int32, (L, G), 1) & (L - 1)
    strict_w = ce < re
    eye_w = (ce == re).astype(F32)
    rt = lax.broadcasted_iota(jnp.int32, (tb, tb), 0)
    ct = lax.broadcasted_iota(jnp.int32, (tb, tb), 1)
    tril_chunks = ((ct <= rt) & ((ct >> hshift) == (rt >> hshift))).astype(BF16)
    assert L == RWKV_HEAD
    heads_per_tile = LANES // RWKV_HEAD
    lane_t = lax.broadcasted_iota(jnp.int32, (L, LANES), 1) >> hshift
    head_pick = [(lane_t == hh).astype(BF16) for hh in range(heads_per_tile)]
    tile_bd = bd_ones[0:LANES, 0:LANES]
    zero_lt = jnp.zeros((L, LANES), BF16)

    def bd(x):
        xb = _bf(x)
        cols = []
        for lt in range(G // LANES):
            tile = xb[:, lt * LANES:(lt + 1) * LANES]
            blocks = [zero_lt] * (G // L)
            for hh in range(heads_per_tile):
                blocks[lt * heads_per_tile + hh] = tile * head_pick[hh]
            cols.append(jnp.concatenate(blocks, axis=0))
        return jnp.concatenate(cols, axis=1)

    def token_shift(x, prev_row, mu_row):
        return x + (_shift_rows(x, prev_row) - x) * mu_row

    units = []
    post = []
    for bi in range(nbat):
        r_raw, k_raw, v_raw, g_raw, lo_raw = (ref[bi].astype(F32)
                                              for ref in (r_ref, k_ref, v_ref, g_ref, lo_ref))
        r = token_shift(r_raw, prev_ref[bi, 0:1, :], mu_ref[0:1, :])
        k = token_shift(k_raw, prev_ref[bi, 1:2, :], mu_ref[1:2, :])
        v = token_shift(v_raw, prev_ref[bi, 2:3, :], mu_ref[2:3, :])
        g = token_shift(g_raw, prev_ref[bi, 3:4, :], mu_ref[3:4, :])
        lo = token_shift(lo_raw, prevlo_ref[bi], mulo_ref[...])
        prev_ref[bi, 0:1, :] = r_raw[tb - 1:tb, :]
        prev_ref[bi, 1:2, :] = k_raw[tb - 1:tb, :]
        prev_ref[bi, 2:3, :] = v_raw[tb - 1:tb, :]
        prev_ref[bi, 3:4, :] = g_raw[tb - 1:tb, :]
        prevlo_ref[bi] = lo_raw[tb - 1:tb, :]

        lane_lo = lax.broadcasted_iota(jnp.int32, lo.shape, 1)
        lo_in = _bf(jnp.where(lane_lo < LORA, jnp.tanh(lo), lo))
        dwa = _dot(lo_in, w2a_ref[...])
        ld = (-math.exp(-0.5)) * _sigmoid(w0_ref[...] + dwa[:, :w])
        a = _sigmoid(a0_ref[...] + dwa[:, w:])

        kk = k * kk_ref[...]
        kk = kk * lax.rsqrt(jnp.maximum(_seg64_sum(kk * kk, bd_ones), 1e-24))
        kv = k * (1.0 + (a - 1.0) * ka_ref[...])
        b = kk * a

        ld_hi, ld_lo = _split_hi_lo(ld)
        cum = _dot(tril_chunks, ld_hi) + _dot(tril_chunks, ld_lo)
        p_last = [jnp.exp(cum[(c + 1) * L - 1:(c + 1) * L, :]) for c in range(n_chunks)]
        p_inc = jnp.exp(cum)
        p_exc = jnp.exp(cum - ld)
        p_inv = jnp.exp(-cum)
        p_hat = p_inv * jnp.concatenate([jnp.broadcast_to(pl_, (L, w)) for pl_ in p_last], axis=0)

        kap_t = kk * p_exc
        r_t = r * p_inc
        b_t = b * p_inv
        k_t = kv * p_inv
        b_h = b * p_hat
        k_h = kv * p_hat
        post.append((r, kv, v, g))
        for c in range(n_chunks):
            rows = slice(c * L, (c + 1) * L)
            for gi in range(n_groups):
                cols = slice(gi * G, (gi + 1) * G)
                units.append(dict(
                    c=c, bi=bi, gi=gi,
                    gq=_bf(jnp.concatenate([kap_t[rows, cols], r_t[rows, cols]], axis=0)),
                    b_t=b_t[rows, cols], k_t=k_t[rows, cols], v=v[rows, cols],
                    zh=jnp.concatenate([b_h[rows, cols], k_h[rows, cols]], axis=0),
                    p_last=p_last[c][:, cols]))

    for u in units:
        scb = _dot_nt(u["gq"], bd(u["b_t"]))
        sck = _dot_nt(u["gq"], bd(u["k_t"]))
        u["pw"] = jnp.where(strict_w, -scb[0:L], 0.0)
        u["r_b"] = _bf(scb[L:2 * L]) * score_mask[L:2 * L]
        u["ak_rk"] = _bf(sck) * score_mask
        u["tinv"] = eye_w + u["pw"]
    for u in units:
        akrk = _dot(u["ak_rk"], bd(u["v"]))
        u["akv"] = akrk[0:L]
        u["rkv"] = akrk[L:2 * L]
        u["zh_t"] = _bf(u["zh"].T)
        u["p_col"] = jnp.broadcast_to(u["p_last"], (LANES, G)).T
    for u in units:
        u["pw_next"] = _dot(_bf(u["pw"]), bd(u["pw"]))
    for _ in range(int(math.log2(L)) - 2):
        for u in units:
            u["pw"] = u["pw_next"]
            both = _dot(_bf(jnp.concatenate([u["tinv"], u["pw"]], axis=0)), bd(u["pw"]))
            u["tinv"] = u["tinv"] + both[0:L]
            u["pw_next"] = both[L:2 * L]
    for u in units:
        u["tinv"] = u["tinv"] + _dot(_bf(u["tinv"]), bd(u["pw_next"]))

    ys = {}
    n_q = G // LANES
    zero_q = jnp.zeros((LANES, LANES), BF16)
    for c in range(n_chunks):
        cu = [u for u in units if u["c"] == c]
        for u in cu:
            st = [state_ref[u["bi"], u["gi"], q] for q in range(n_q)]
            u["st"] = st
            st_bd = jnp.concatenate(
                [jnp.concatenate([_bf(st[q]) * tile_bd if qq == q else zero_q for qq in range(n_q)],
                                 axis=1) for q in range(n_q)], axis=0)
            x0 = _dot(u["gq"], st_bd)
            u["rhs"] = -(x0[0:L] + u["akv"])
            u["y"] = x0[L:2 * L] + u["rkv"]
        for u in cu:
            u["u"] = _dot(_bf(u["tinv"]), bd(u["rhs"]))
        for u in cu:
            ys[(u["bi"], c, u["gi"])] = u["y"] + _dot(u["r_b"], bd(u["u"]))
            uv = _bf(jnp.concatenate([u["u"], u["v"]], axis=0))
            for q in range(n_q):
                qs = slice(q * LANES, (q + 1) * LANES)
                state_ref[u["bi"], u["gi"], q] = (u["st"][q] * u["p_col"][qs, :]
                                                  + _dot(u["zh_t"][qs, :], uv[:, qs]))

    for bi in range(nbat):
        r, kv, v, g = post[bi]
        y = jnp.concatenate(
            [jnp.concatenate([ys[(bi, c, gi)] for gi in range(n_groups)], axis=1)
             for c in range(n_chunks)], axis=0)
        mean = _seg64_sum(y, bd_ones) * (1.0 / RWKV_HEAD)
        d = y - mean
        var = _seg64_sum(d * d, bd_ones) * (1.0 / RWKV_HEAD)
        yn = d * lax.rsqrt(var + RWKV_GN_EPS) * lnw_ref[...] + lnb_ref[...]
        bonus = _seg64_sum(r * kv * rk_ref[...], bd_ones) * v
        o_ref[bi] = ((yn + bonus) * (g * _sigmoid(g))).astype(o_ref.dtype)


def _rwkv_branch(p, mu, w2a_bf, w0, a0, k_k, k_a, r_k, lnx_w, lnx_b, bsz, s):
    w = BRANCH_W
    tb = min(RWKV_TB, s)
    nb = s // tb
    p3 = p.reshape(bsz, s, p.shape[1])
    row = lambda x: x.reshape(1, -1)
    mu4 = mu[:4 * w].reshape(4, w)
    mulo = mu[4 * w:].reshape(1, 2 * LORA)
    tok = lambda c: pl.BlockSpec((bsz, tb, w), lambda i, c=c: (0, i, P_OFF_B // w + c))
    par = lambda shp: pl.BlockSpec(shp, lambda i: (0,) * len(shp))
    out = pl.pallas_call(
        _rwkv_kernel,
        out_shape=jax.ShapeDtypeStruct((bsz, s, w), BF16),
        grid=(nb,),
        in_specs=[tok(0), tok(1), tok(2), tok(3),
                  pl.BlockSpec((bsz, tb, 2 * LORA), lambda i: (0, i, (P_OFF_B + 4 * w) // (2 * LORA))),
                  par((4, w)), par((1, 2 * LORA)), par((2 * LORA, 2 * w)),
                  par((1, w)), par((1, w)), par((1, w)), par((1, w)), par((1, w)),
                  par((1, w)), par((1, w))],
        out_specs=pl.BlockSpec((bsz, tb, w), lambda i: (0, i, 0)),
        scratch_shapes=[pltpu.VMEM((bsz, 4, w), F32), pltpu.VMEM((bsz, 1, 2 * LORA), F32),
                        pltpu.VMEM((bsz, w // MXU_DIM, MXU_DIM // LANES, LANES, LANES), F32)],
        compiler_params=_cparams(1),
        name="rwkv7",
    )(p3, p3, p3, p3, p3, mu4, mulo, w2a_bf, row(w0), row(a0), row(k_k), row(k_a),
      row(r_k), row(lnx_w), row(lnx_b))
    return out.reshape(bsz * s, w)


def _ret_kernel(q_ref, k_ref, v_ref, g_ref, cos_ref, sin_ref, o_ref, state_ref):
    C = q_ref.shape[0]
    dh = RET_HEAD
    half = dh // 2
    i = pl.program_id(1)

    @pl.when(i == 0)
    def _():
        state_ref[...] = jnp.zeros_like(state_ref)

    cos = cos_ref[...]
    sin = sin_ref[...]
    ri = lax.broadcasted_iota(jnp.int32, (C, C), 0)
    ci = lax.broadcasted_iota(jnp.int32, (C, C), 1)
    rel = (ri - ci).astype(F32)
    rowf = lax.broadcasted_iota(jnp.int32, (C, dh), 0).astype(F32)

    def rot(t):
        t1, t2 = t[:, :half], t[:, half:]
        return jnp.concatenate([t1 * cos - t2 * sin, t1 * sin + t2 * cos], axis=1)

    outs = []
    for h in range(RET_HEADS):
        sl = slice(h * dh, (h + 1) * dh)
        log_g = math.log1p(-(2.0 ** (-5.0 - h)))
        q = rot(q_ref[:, sl].astype(F32))
        k = rot(k_ref[:, sl].astype(F32)) * (dh ** -0.5)
        v_bf = v_ref[:, sl]
        mask = jnp.where(rel >= 0.0, jnp.exp(log_g * jnp.maximum(rel, 0.0)), 0.0)
        scores = _dot_nt(q.astype(BF16), k.astype(BF16)) * mask
        inner = _dot(scores.astype(BF16), v_bf)
        q_dec = jnp.exp(log_g * (rowf + 1.0))
        k_dec = jnp.exp(log_g * (C - 1.0 - rowf))
        st = state_ref[h]
        cross = _dot((q * q_dec).astype(BF16), st.astype(BF16))
        state_ref[h] = st * math.exp(log_g * C) + _dot((k * k_dec).T.astype(BF16), v_bf)
        out = inner + cross
        mu = jnp.mean(out, axis=-1, keepdims=True)
        dlt = out - mu
        var = jnp.mean(dlt * dlt, axis=-1, keepdims=True)
        outs.append(dlt * lax.rsqrt(var + RET_GN_EPS))
    out = jnp.concatenate(outs, axis=1)
    g = g_ref[...].astype(F32)
    o_ref[...] = (out * (g * _sigmoid(g))).astype(o_ref.dtype)


def _ret_branch(p, cos_t, sin_t, bsz, s):
    w = BRANCH_W
    C = RET_CHUNK
    nb = s // C
    tok = lambda c: pl.BlockSpec((C, w), lambda b, i, c=c: (b * nb + i, P_OFF_C // w + c))
    tab = pl.BlockSpec((C, RET_HEAD // 2), lambda b, i: (i, 0))
    return pl.pallas_call(
        _ret_kernel,
        out_shape=jax.ShapeDtypeStruct((bsz * s, w), BF16),
        grid=(bsz, nb),
        in_specs=[tok(0), tok(1), tok(2), tok(3), tab, tab],
        out_specs=pl.BlockSpec((C, w), lambda b, i: (b * nb + i, 0)),
        scratch_shapes=[pltpu.VMEM((RET_HEADS, RET_HEAD, RET_HEAD), F32)],
        compiler_params=_cparams(2),
        name="retention",
    )(p, p, p, p, cos_t, sin_t)


def _merge_kernel(x_ref, ya_ref, yb_ref, yc_ref, pm_ref, bm_ref, wb_ref, wo_ref, fw_ref, o_ref,
                  *, final_norm):
    d = x_ref.shape[1]
    merged = None
    for n, y_ref in enumerate((ya_ref, yb_ref, yc_ref)):
        if n == 0:
            proj = sum(_dot(y_ref[blk], wb_ref[0, blk * LRU_BLOCK:(blk + 1) * LRU_BLOCK, :])
                       for blk in range(LRU_BLOCKS))
        else:
            proj = _dot(y_ref[...], wb_ref[n])
        gate = _sigmoid(pm_ref[:, n * d:(n + 1) * d].astype(F32) + bm_ref[:, n * d:(n + 1) * d])
        merged = gate * proj if merged is None else merged + gate * proj
    out = x_ref[...] + _dot(merged.astype(BF16), wo_ref[...])
    if final_norm:
        ms = jnp.mean(out * out, axis=-1, keepdims=True)
        out = out * lax.rsqrt(ms + EPS) * fw_ref[...]
    o_ref[...] = out


def _merge(x2, y_a, y_b, y_c, p, b_merge, w_branch_bf, w_out_bf, final_w, final_norm, tm):
    t, d = x2.shape
    w = BRANCH_W
    rows = lambda n: pl.BlockSpec((tm, n), lambda i: (i, 0))
    return pl.pallas_call(
        functools.partial(_merge_kernel, final_norm=final_norm),
        out_shape=jax.ShapeDtypeStruct((t, d), F32),
        grid=(t // tm,),
        in_specs=[rows(d), pl.BlockSpec((LRU_BLOCKS, tm, LRU_BLOCK), lambda i: (0, i, 0)),
                  rows(w), rows(w),
                  pl.BlockSpec((tm, 3 * d), lambda i: (i, P_OFF_M // (3 * d))),
                  pl.BlockSpec((1, 3 * d), lambda i: (0, 0)),
                  pl.BlockSpec((3, w, d), lambda i: (0, 0, 0)),
                  pl.BlockSpec((d, d), lambda i: (0, 0)),
                  pl.BlockSpec((1, d), lambda i: (0, 0))],
        out_specs=rows(d),
        compiler_params=_cparams(1),
        name="merge",
    )(x2, y_a, y_b, y_c, p, b_merge.reshape(1, 3 * d), w_branch_bf, w_out_bf,
      final_w.reshape(1, d))


def _rope_tables(s):
    half = RET_HEAD // 2
    inv_freq = 1.0 / (ROPE_BASE ** jnp.linspace(0.0, 1.0, half, dtype=F32))
    ang = jnp.arange(s)[:, None].astype(F32) * inv_freq
    return jnp.cos(ang), jnp.sin(ang)


def _lora_block(decay_w2, iclr_a2):
    z = jnp.zeros((LORA, BRANCH_W), F32)
    return jnp.concatenate([jnp.concatenate([decay_w2, z], axis=1),
                            jnp.concatenate([z, iclr_a2], axis=1)], axis=0).astype(BF16)


def kernel(x, norm_w, w_in, b_merge, conv_w, conv_b, lru_gate_w, lru_gate_b, lru_lambda, shift_mu,
           decay_w0, decay_w2, iclr_a0, iclr_a2, k_k, k_a, r_k, lnx_w, lnx_b, w_branch, w_out,
           final_norm_w):
    bsz, s, d = x.shape
    depth = norm_w.shape[0]
    t = bsz * s
    tm = min(512, t)
    cos_t, sin_t = _rope_tables(s)
    x2 = x.reshape(t, d)
    for l in range(depth):
        w = w_in[l]
        w_main = jnp.concatenate([w[:, OFF_M:], w[:, OFF_C:OFF_M], w[:, OFF_B:OFF_C],
                                  jnp.zeros((d, P_PAD), F32)], axis=1).astype(BF16)
        p, y_a = _proj_lru(x2, norm_w[l], w_main, w[:, :OFF_B].astype(BF16), conv_w[l], conv_b[l],
                           lru_gate_w[l].astype(BF16), lru_gate_b[l], lru_lambda[l], s)
        y_b = _rwkv_branch(p, shift_mu[l], _lora_block(decay_w2[l], iclr_a2[l]), decay_w0[l],
                           iclr_a0[l], k_k[l], k_a[l], r_k[l].reshape(-1), lnx_w[l], lnx_b[l],
                           bsz, s)
        y_c = _ret_branch(p, cos_t, sin_t, bsz, s)
        x2 = _merge(x2, y_a, y_b, y_c, p, b_merge[l], w_branch[l].astype(BF16),
                    w_out[l].astype(BF16), final_norm_w, l == depth - 1, tm)
    return x2.reshape(bsz, s, d)
```

```python
import functools
import math

import jax
import jax.numpy as jnp
import numpy as np
from jax import lax
from jax.experimental import pallas as pl
from jax.experimental.pallas import tpu as pltpu

F32 = jnp.float32
BF16 = jnp.bfloat16

D_MODEL = 1024
EPS = 1e-6
BRANCH_W = 1024
LRU_BLOCKS = 4
LRU_BLOCK = BRANCH_W // LRU_BLOCKS
CONV_W = 4
LRU_C = 8.0
RWKV_HEAD = 64
LORA = 64
RWKV_GN_EPS = 64e-5
RET_HEADS = 4
RET_HEAD = BRANCH_W // RET_HEADS
RET_CHUNK = 128
ROPE_BASE = 10000.0
RET_GN_EPS = 1e-5
B_COLS = 4 * BRANCH_W + 2 * LORA
OFF_B = 2 * BRANCH_W
OFF_C = OFF_B + B_COLS
OFF_M = OFF_C + 4 * BRANCH_W
N_IN = OFF_M + 3 * D_MODEL
LANES = 128
SUBLANES = 8
MXU_DIM = 256
RWKV_CHUNK = 64
RWKV_TB = 128
PROJ_TM = 512
LRU_PIECE_ROWS = 64
P_OFF_M = 0
P_OFF_C = P_OFF_M + 3 * D_MODEL
P_OFF_B = P_OFF_C + 4 * BRANCH_W
PROJ_TN = -(-(P_OFF_B + B_COLS) // (LRU_BLOCKS * MXU_DIM)) * MXU_DIM
P_COLS = PROJ_TN * LRU_BLOCKS
P_PAD = P_COLS - (P_OFF_B + B_COLS)
VMEM_LIMIT = 48 * 1024 * 1024


def _cparams(n_axes):
    return pltpu.CompilerParams(dimension_semantics=("arbitrary",) * n_axes,
                                vmem_limit_bytes=VMEM_LIMIT)


def _sigmoid(x):
    return 1.0 / (1.0 + jnp.exp(-x))


def _softplus(x):
    return jnp.maximum(x, 0.0) + jnp.log(1.0 + jnp.exp(-jnp.abs(x)))


def _dot(a, b):
    return jnp.dot(a, b, preferred_element_type=F32)


def _dot_nt(a, b):
    return lax.dot_general(a, b, (((1,), (1,)), ((), ())), preferred_element_type=F32)


def _bf(x):
    return x.astype(BF16)


def _split_hi_lo(x):
    hi = x.astype(BF16)
    lo = (x - hi.astype(F32)).astype(BF16)
    return hi, lo


def _lru_pieces(xg_ref, n, cw, cb, gw, gb, lam, tail, h, first, rows_per_piece, carry_out):
    tm, w = xg_ref.shape[2], xg_ref.shape[3]
    tail = jnp.where(first, 0.0, tail)
    h = jnp.where(first, 0.0, h)
    decay_rate = _softplus(-lam)
    sub = lax.broadcasted_iota(jnp.int32, (SUBLANES, w), 0)
    row = lax.broadcasted_iota(jnp.int32, (rows_per_piece, w), 0)
    for piece in range(tm // rows_per_piece):
        rows = slice(piece * rows_per_piece, (piece + 1) * rows_per_piece)
        x = xg_ref[0, n, rows, :]
        ga = xg_ref[1, n, rows, :]
        xfull = jnp.concatenate([tail, x], axis=0)
        tail = x[rows_per_piece - SUBLANES:, :]
        conv = cb + x * cw[CONV_W - 1:CONV_W, :]
        for s in range(1, CONV_W):
            shifted = pltpu.roll(xfull, s, 0)[SUBLANES:, :]
            conv = conv + shifted * cw[CONV_W - 1 - s:CONV_W - s, :]
        conv_bf = _bf(conv)
        r = _sigmoid(_dot(conv_bf, gw[0]) + gb[0:1, :])
        ig = _sigmoid(_dot(conv_bf, gw[1]) + gb[1:2, :])
        log_a = (-LRU_C) * r * decay_rate
        a = jnp.exp(log_a)
        z = -jnp.tanh(log_a) * (a * a + 1.0)
        mult = jnp.where(z > 0.0, z * lax.rsqrt(z), 0.0)
        if piece == 0:
            mult = jnp.where((row == 0) & first, 1.0, mult)
        b = mult * ig * conv

        tiles = []
        for g in range(rows_per_piece // SUBLANES):
            a8 = a[g * SUBLANES:(g + 1) * SUBLANES, :]
            b8 = b[g * SUBLANES:(g + 1) * SUBLANES, :]
            d = 1
            while d < SUBLANES:
                keep = sub >= d
                a_prev = jnp.where(keep, pltpu.roll(a8, d, 0), 1.0)
                b_prev = jnp.where(keep, pltpu.roll(b8, d, 0), 0.0)
                b8 = a8 * b_prev + b8
                a8 = a8 * a_prev
                d *= 2
            hs = a8 * h + b8
            tiles.append(hs)
            h = hs[SUBLANES - 1:SUBLANES, :]
        carry_out[:] = [tail, h]
        yield rows, jnp.concatenate(tiles, axis=0) * (ga * _sigmoid(ga))


def _proj_lru_kernel(x_ref, nw_ref, w_ref, wl_ref, cw_ref, cb_ref, gw_ref, gb_ref, lam_ref,
                     p_ref, ya_ref, h_ref, xg_ref, tail_ref, hc_ref, *, blocks_per_seq):
    i = pl.program_id(0)
    n = pl.program_id(1)

    @pl.when(n == 0)
    def _():
        x = x_ref[...]
        ms = jnp.mean(x * x, axis=-1, keepdims=True)
        h = _bf(x * lax.rsqrt(ms + EPS) * nw_ref[...])
        h_ref[...] = h
        xg = _dot(h, wl_ref[...])
        for half in range(2):
            for blk in range(LRU_BLOCKS):
                c0 = half * BRANCH_W + blk * LRU_BLOCK
                xg_ref[half, blk] = xg[:, c0:c0 + LRU_BLOCK]

    first = (i % blocks_per_seq) == 0
    carry = []
    pieces = _lru_pieces(xg_ref, n, cw_ref[n], cb_ref[n], gw_ref[n], gb_ref[n], lam_ref[n],
                         tail_ref[n], hc_ref[n], first, LRU_PIECE_ROWS, carry)
    h = h_ref[...]
    for c in range(p_ref.shape[1] // MXU_DIM):
        cols = slice(c * MXU_DIM, (c + 1) * MXU_DIM)
        p_ref[:, cols] = _dot(h, w_ref[:, cols]).astype(p_ref.dtype)
        piece = next(pieces, None)
        if piece is not None:
            rows, y = piece
            ya_ref[n, rows, :] = y.astype(ya_ref.dtype)
    assert next(pieces, None) is None
    tail_ref[n], hc_ref[n] = carry


def _proj_lru(x2, norm_w, w_main_bf, w_lru_bf, conv_w, conv_b, gate_w_bf, gate_b, lam, s):
    t, d = x2.shape
    tm = min(PROJ_TM, s)
    nblk, blk = LRU_BLOCKS, LRU_BLOCK
    per_block = lambda a: a.reshape(a.shape[0], nblk, blk).transpose(1, 0, 2)
    par = lambda shp: pl.BlockSpec(shp, lambda i, j: (0,) * len(shp))
    return pl.pallas_call(
        functools.partial(_proj_lru_kernel, blocks_per_seq=s // tm),
        out_shape=(jax.ShapeDtypeStruct((t, P_COLS), BF16),
                   jax.ShapeDtypeStruct((nblk, t, blk), BF16)),
        grid=(t // tm, nblk),
        in_specs=[pl.BlockSpec((tm, d), lambda i, j: (i, 0)),
                  par((1, d)),
                  pl.BlockSpec((d, PROJ_TN), lambda i, j: (0, j)),
                  par((d, 2 * BRANCH_W)),
                  par((nblk, CONV_W, blk)), par((nblk, 1, blk)), par((nblk, 2, blk, blk)),
                  par((nblk, 2, blk)), par((nblk, 1, blk))],
        out_specs=(pl.BlockSpec((tm, PROJ_TN), lambda i, j: (i, j)),
                   pl.BlockSpec((nblk, tm, blk), lambda i, j: (0, i, 0))),
        scratch_shapes=[pltpu.VMEM((tm, d), BF16),
                        pltpu.VMEM((2, nblk, tm, blk), F32),
                        pltpu.VMEM((nblk, SUBLANES, blk), F32),
                        pltpu.VMEM((nblk, 1, blk), F32)],
        compiler_params=_cparams(2),
        name="proj_rglru",
    )(x2, norm_w.reshape(1, d), w_main_bf, w_lru_bf, per_block(conv_w),
      per_block(conv_b.reshape(1, -1)), gate_w_bf.transpose(1, 0, 2, 3), per_block(gate_b),
      per_block(lam.reshape(1, -1)))


def _seg64_sum(x, bd_ones):
    xb = _bf(x)
    out = [_dot(xb[:, j * MXU_DIM:(j + 1) * MXU_DIM], bd_ones) for j in range(x.shape[1] // MXU_DIM)]
    return jnp.concatenate(out, axis=1)


def _shift_rows(x, prev_row):
    rolled = pltpu.roll(x, 1, 0)
    head = rolled[0:SUBLANES, :]
    row = lax.broadcasted_iota(jnp.int32, head.shape, 0)
    head = jnp.where(row == 0, jnp.broadcast_to(prev_row, head.shape), head)
    return jnp.concatenate([head, rolled[SUBLANES:, :]], axis=0)


def _rwkv_kernel(r_ref, k_ref, v_ref, g_ref, lo_ref, mu_ref, mulo_ref, w2a_ref, w0_ref, a0_ref,
                 kk_ref, ka_ref, rk_ref, lnw_ref, lnb_ref, o_ref, prev_ref, prevlo_ref, state_ref):
    nbat, tb, w = r_ref.shape
    L = RWKV_CHUNK
    G = MXU_DIM
    n_chunks = tb // L
    n_groups = w // G
    hshift = RWKV_HEAD.bit_length() - 1

    @pl.when(pl.program_id(0) == 0)
    def _():
        prev_ref[...] = jnp.zeros_like(prev_ref)
        prevlo_ref[...] = jnp.zeros_like(prevlo_ref)
        state_ref[...] = jnp.zeros_like(state_ref)

    rg = lax.broadcasted_iota(jnp.int32, (G, G), 0)
    cg = lax.broadcasted_iota(jnp.int32, (G, G), 1)
    same_head = (rg >> hshift) == (cg >> hshift)
    bd_ones = same_head.astype(BF16)
    r2 = lax.broadcasted_iota(jnp.int32, (2 * L, G), 0)
    c2 = lax.broadcasted_iota(jnp.int32, (2 * L, G), 1) & (L - 1)
    score_mask = (c2 < jnp.where(r2 >= L, r2 - (L - 1), r2)).astype(BF16)
    re = lax.broadcasted_iota(jnp.int32, (L, G), 0)
    ce = lax.broadcasted_iota(jnp.int32, (L, G), 1) & (L - 1)
    strict_w = ce < re
    eye_w = (ce == re).astype(F32)
    rt = lax.broadcasted_iota(jnp.int32, (tb, tb), 0)
    ct = lax.broadcasted_iota(jnp.int32, (tb, tb), 1)
    tril_chunks = ((ct <= rt) & ((ct >> hshift) == (rt >> hshift))).astype(BF16)
    tril2 = jnp.concatenate([tril_chunks, tril_chunks], axis=1)
    assert L == RWKV_HEAD
    heads_per_tile = LANES // RWKV_HEAD
    lane_t = lax.broadcasted_iota(jnp.int32, (L, LANES), 1) >> hshift
    head_pick = [(lane_t == hh).astype(BF16) for hh in range(heads_per_tile)]
    tile_bd = bd_ones[0:LANES, 0:LANES]
    zero_lt = jnp.zeros((L, LANES), BF16)

    def bd(x):
        xb = _bf(x)
        cols = []
        for lt in range(G // LANES):
            tile = xb[:, lt * LANES:(lt + 1) * LANES]
            blocks = [zero_lt] * (G // L)
            for hh in range(heads_per_tile):
                blocks[lt * heads_per_tile + hh] = tile * head_pick[hh]
            cols.append(jnp.concatenate(blocks, axis=0))
        return jnp.concatenate(cols, axis=1)

    def token_shift(x, prev_row, mu_row):
        return x + (_shift_rows(x, prev_row) - x) * mu_row

    units = []
    post = []
    for bi in range(nbat):
        r_raw, k_raw, v_raw, g_raw, lo_raw = (ref[bi].astype(F32)
                                              for ref in (r_ref, k_ref, v_ref, g_ref, lo_ref))
        r = token_shift(r_raw, prev_ref[bi, 0:1, :], mu_ref[0:1, :])
        k = token_shift(k_raw, prev_ref[bi, 1:2, :], mu_ref[1:2, :])
        v = token_shift(v_raw, prev_ref[bi, 2:3, :], mu_ref[2:3, :])
        g = token_shift(g_raw, prev_ref[bi, 3:4, :], mu_ref[3:4, :])
        lo = token_shift(lo_raw, prevlo_ref[bi], mulo_ref[...])
        prev_ref[bi, 0:1, :] = r_raw[tb - 1:tb, :]
        prev_ref[bi, 1:2, :] = k_raw[tb - 1:tb, :]
        prev_ref[bi, 2:3, :] = v_raw[tb - 1:tb, :]
        prev_ref[bi, 3:4, :] = g_raw[tb - 1:tb, :]
        prevlo_ref[bi] = lo_raw[tb - 1:tb, :]

        lane_lo = lax.broadcasted_iota(jnp.int32, lo.shape, 1)
        lo_in = _bf(jnp.where(lane_lo < LORA, jnp.tanh(lo), lo))
        dwa = _dot(lo_in, w2a_ref[...])
        ld = (-math.exp(-0.5)) * _sigmoid(w0_ref[...] + dwa[:, :w])
        a = _sigmoid(a0_ref[...] + dwa[:, w:])

        kk = k * kk_ref[...]
        kk = kk * lax.rsqrt(jnp.maximum(_seg64_sum(kk * kk, bd_ones), 1e-24))
        kv = k * (1.0 + (a - 1.0) * ka_ref[...])
        b = kk * a

        cum = _dot(tril2, jnp.concatenate(_split_hi_lo(ld), axis=0))
        p_last = [jnp.exp(cum[(c + 1) * L - 1:(c + 1) * L, :]) for c in range(n_chunks)]
        p_inc = jnp.exp(cum)
        p_exc = jnp.exp(cum - ld)
        p_inv = jnp.exp(-cum)
        p_hat = p_inv * jnp.concatenate([jnp.broadcast_to(pl_, (L, w)) for pl_ in p_last], axis=0)

        kap_t = kk * p_exc
        r_t = r * p_inc
        b_t = b * p_inv
        k_t = kv * p_inv
        b_h = b * p_hat
        k_h = kv * p_hat
        post.append((r, kv, v, g))
        for c in range(n_chunks):
            rows = slice(c * L, (c + 1) * L)
            for gi in range(n_groups):
                cols = slice(gi * G, (gi + 1) * G)
                units.append(dict(
                    c=c, bi=bi, gi=gi,
                    gq=_bf(jnp.concatenate([kap_t[rows, cols], r_t[rows, cols]], axis=0)),
                    b_t=b_t[rows, cols], k_t=k_t[rows, cols], v=v[rows, cols],
                    zh=jnp.concatenate([b_h[rows, cols], k_h[rows, cols]], axis=0),
                    p_last=p_last[c][:, cols]))

    for u in units:
        scb = _dot_nt(u["gq"], bd(u["b_t"]))
        sck = _dot_nt(u["gq"], bd(u["k_t"]))
        u["pw"] = jnp.where(strict_w, -scb[0:L], 0.0)
        u["r_b"] = _bf(scb[L:2 * L]) * score_mask[L:2 * L]
        u["ak_rk"] = _bf(sck) * score_mask
        u["tinv"] = eye_w + u["pw"]
    for u in units:
        akrk = _dot(u["ak_rk"], bd(u["v"]))
        u["akv"] = akrk[0:L]
        u["rkv"] = akrk[L:2 * L]
        u["zh_t"] = _bf(u["zh"].T)
        u["p_col"] = jnp.broadcast_to(u["p_last"], (LANES, G)).T
    for u in units:
        u["pw_next"] = _dot(_bf(u["pw"]), bd(u["pw"]))
    for _ in range(int(math.log2(L)) - 2):
        for u in units:
            u["pw"] = u["pw_next"]
            both = _dot(_bf(jnp.concatenate([u["tinv"], u["pw"]], axis=0)), bd(u["pw"]))
            u["tinv"] = u["tinv"] + both[0:L]
            u["pw_next"] = both[L:2 * L]
    for u in units:
        u["tinv"] = u["tinv"] + _dot(_bf(u["tinv"]), bd(u["pw_next"]))

    ys = {}
    n_q = G // LANES
    zero_q = jnp.zeros((LANES, LANES), BF16)
    for c in range(n_chunks):
        cu = [u for u in units if u["c"] == c]
        for u in cu:
            st = [state_ref[u["bi"], u["gi"], q] for q in range(n_q)]
            u["st"] = st
            st_bd = jnp.concatenate(
                [jnp.concatenate([_bf(st[q]) * tile_bd if qq == q else zero_q for qq in range(n_q)],
                                 axis=1) for q in range(n_q)], axis=0)
            x0 = _dot(u["gq"], st_bd)
            u["rhs"] = -(x0[0:L] + u["akv"])
            u["y"] = x0[L:2 * L] + u["rkv"]
        for u in cu:
            u["u"] = _dot(_bf(u["tinv"]), bd(u["rhs"]))
        for u in cu:
            ys[(u["bi"], c, u["gi"])] = u["y"] + _dot(u["r_b"], bd(u["u"]))
            uv = _bf(jnp.concatenate([u["u"], u["v"]], axis=0))
            for q in range(n_q):
                qs = slice(q * LANES, (q + 1) * LANES)
                state_ref[u["bi"], u["gi"], q] = (u["st"][q] * u["p_col"][qs, :]
                                                  + _dot(u["zh_t"][qs, :], uv[:, qs]))

    for bi in range(nbat):
        r, kv, v, g = post[bi]
        y = jnp.concatenate(
            [jnp.concatenate([ys[(bi, c, gi)] for gi in range(n_groups)], axis=1)
             for c in range(n_chunks)], axis=0)
        mean = _seg64_sum(y, bd_ones) * (1.0 / RWKV_HEAD)
        d = y - mean
        var = _seg64_sum(d * d, bd_ones) * (1.0 / RWKV_HEAD)
        yn = d * lax.rsqrt(var + RWKV_GN_EPS) * lnw_ref[...] + lnb_ref[...]
        bonus = _seg64_sum(r * kv * rk_ref[...], bd_ones) * v
        o_ref[bi] = ((yn + bonus) * (g * _sigmoid(g))).astype(o_ref.dtype)


def _rwkv_branch(p3, mu, w2a_bf, w0, a0, k_k, k_a, r_k, lnx_w, lnx_b):
    bsz, s, _ = p3.shape
    w = BRANCH_W
    tb = min(RWKV_TB, s)
    nb = s // tb
    row = lambda x: x.reshape(1, -1)
    mu4 = mu[:4 * w].reshape(4, w)
    mulo = mu[4 * w:].reshape(1, 2 * LORA)
    tok = lambda c: pl.BlockSpec((bsz, tb, w), lambda i, c=c: (0, i, P_OFF_B // w + c))
    par = lambda shp: pl.BlockSpec(shp, lambda i: (0,) * len(shp))
    out = pl.pallas_call(
        _rwkv_kernel,
        out_shape=jax.ShapeDtypeStruct((bsz, s, w), BF16),
        grid=(nb,),
        in_specs=[tok(0), tok(1), tok(2), tok(3),
                  pl.BlockSpec((bsz, tb, 2 * LORA), lambda i: (0, i, (P_OFF_B + 4 * w) // (2 * LORA))),
                  par((4, w)), par((1, 2 * LORA)), par((2 * LORA, 2 * w)),
                  par((1, w)), par((1, w)), par((1, w)), par((1, w)), par((1, w)),
                  par((1, w)), par((1, w))],
        out_specs=pl.BlockSpec((bsz, tb, w), lambda i: (0, i, 0)),
        scratch_shapes=[pltpu.VMEM((bsz, 4, w), F32), pltpu.VMEM((bsz, 1, 2 * LORA), F32),
                        pltpu.VMEM((bsz, w // MXU_DIM, MXU_DIM // LANES, LANES, LANES), F32)],
        compiler_params=_cparams(1),
        name="rwkv7",
    )(p3, p3, p3, p3, p3, mu4, mulo, w2a_bf, row(w0), row(a0), row(k_k), row(k_a),
      row(r_k), row(lnx_w), row(lnx_b))
    return out.reshape(bsz * s, w)


def _ret_kernel(q_ref, k_ref, v_ref, g_ref, cos_ref, sin_ref, o_ref,
                state_ref, mask_ref, qdec_ref, kdec_ref):
    nbat, C, _ = q_ref.shape
    dh = RET_HEAD
    half = dh // 2
    log_g = [math.log1p(-(2.0 ** (-5.0 - h))) for h in range(RET_HEADS)]

    @pl.when(pl.program_id(0) == 0)
    def _():
        state_ref[...] = jnp.zeros_like(state_ref)
        ri = lax.broadcasted_iota(jnp.int32, (C, C), 0)
        ci = lax.broadcasted_iota(jnp.int32, (C, C), 1)
        rel = (ri - ci).astype(F32)
        rowf = lax.broadcasted_iota(jnp.int32, (C, dh), 0).astype(F32)
        for h in range(RET_HEADS):
            mask_ref[h] = jnp.where(rel >= 0.0, jnp.exp(log_g[h] * jnp.maximum(rel, 0.0)),
                                    0.0) * (dh ** -0.5)
            qdec_ref[h] = jnp.exp(log_g[h] * (rowf + 1.0))
            kdec_ref[h] = jnp.exp(log_g[h] * (C - 1.0 - rowf)) * (dh ** -0.5)

    cos = cos_ref[...]
    sin = sin_ref[...]

    def rot(t):
        t1, t2 = t[:, :half], t[:, half:]
        return jnp.concatenate([t1 * cos - t2 * sin, t1 * sin + t2 * cos], axis=1)

    outs = [[] for _ in range(nbat)]
    for h in range(RET_HEADS):
        sl = slice(h * dh, (h + 1) * dh)
        for bi in range(nbat):
            q = rot(q_ref[bi, :, sl].astype(F32))
            k = rot(k_ref[bi, :, sl].astype(F32))
            v_bf = v_ref[bi, :, sl]
            scores = _dot_nt(_bf(q), _bf(k)) * mask_ref[h]
            inner = _dot(_bf(scores), v_bf)
            st = state_ref[bi, h]
            cross = _dot(_bf(q * qdec_ref[h]), _bf(st))
            state_ref[bi, h] = st * math.exp(log_g[h] * C) + _dot(_bf((k * kdec_ref[h]).T), v_bf)
            out = inner + cross
            mu = jnp.mean(out, axis=-1, keepdims=True)
            dlt = out - mu
            var = jnp.mean(dlt * dlt, axis=-1, keepdims=True)
            outs[bi].append(dlt * lax.rsqrt(var + RET_GN_EPS))
    for bi in range(nbat):
        g = g_ref[bi].astype(F32)
        o_ref[bi] = (jnp.concatenate(outs[bi], axis=1) * (g * _sigmoid(g))).astype(o_ref.dtype)


def _ret_branch(p3, cos_t, sin_t):
    bsz, s, _ = p3.shape
    w = BRANCH_W
    C = RET_CHUNK
    tok = lambda c: pl.BlockSpec((bsz, C, w), lambda i, c=c: (0, i, P_OFF_C // w + c))
    tab = pl.BlockSpec((C, RET_HEAD // 2), lambda i: (i, 0))
    out = pl.pallas_call(
        _ret_kernel,
        out_shape=jax.ShapeDtypeStruct((bsz, s, w), BF16),
        grid=(s // C,),
        in_specs=[tok(0), tok(1), tok(2), tok(3), tab, tab],
        out_specs=pl.BlockSpec((bsz, C, w), lambda i: (0, i, 0)),
        scratch_shapes=[pltpu.VMEM((bsz, RET_HEADS, RET_HEAD, RET_HEAD), F32),
                        pltpu.VMEM((RET_HEADS, C, C), F32),
                        pltpu.VMEM((RET_HEADS, C, RET_HEAD), F32),
                        pltpu.VMEM((RET_HEADS, C, RET_HEAD), F32)],
        compiler_params=_cparams(1),
        name="retention",
    )(p3, p3, p3, p3, cos_t, sin_t)
    return out.reshape(bsz * s, w)


def _merge_kernel(x_ref, ya_ref, yb_ref, yc_ref, pm_ref, bm_ref, wb_ref, wo_ref, fw_ref, o_ref,
                  *, final_norm):
    d = x_ref.shape[1]
    merged = None
    for n, y_ref in enumerate((ya_ref, yb_ref, yc_ref)):
        if n == 0:
            proj = sum(_dot(y_ref[blk], wb_ref[0, blk * LRU_BLOCK:(blk + 1) * LRU_BLOCK, :])
                       for blk in range(LRU_BLOCKS))
        else:
            proj = _dot(y_ref[...], wb_ref[n])
        gate = _sigmoid(pm_ref[:, n * d:(n + 1) * d].astype(F32) + bm_ref[:, n * d:(n + 1) * d])
        merged = gate * proj if merged is None else merged + gate * proj
    out = x_ref[...] + _dot(merged.astype(BF16), wo_ref[...])
    if final_norm:
        ms = jnp.mean(out * out, axis=-1, keepdims=True)
        out = out * lax.rsqrt(ms + EPS) * fw_ref[...]
    o_ref[...] = out


def _merge(x2, y_a, y_b, y_c, p, b_merge, w_branch_bf, w_out_bf, layer, final_w, final_norm, tm):
    t, d = x2.shape
    w = BRANCH_W
    rows = lambda n: pl.BlockSpec((tm, n), lambda i: (i, 0))
    return pl.pallas_call(
        functools.partial(_merge_kernel, final_norm=final_norm),
        out_shape=jax.ShapeDtypeStruct((t, d), F32),
        grid=(t // tm,),
        in_specs=[rows(d), pl.BlockSpec((LRU_BLOCKS, tm, LRU_BLOCK), lambda i: (0, i, 0)),
                  rows(w), rows(w),
                  pl.BlockSpec((tm, 3 * d), lambda i: (i, P_OFF_M // (3 * d))),
                  pl.BlockSpec((1, 3 * d), lambda i: (0, 0)),
                  pl.BlockSpec((None, 3, w, d), lambda i: (layer, 0, 0, 0)),
                  pl.BlockSpec((None, d, d), lambda i: (layer, 0, 0)),
                  pl.BlockSpec((1, d), lambda i: (0, 0))],
        out_specs=rows(d),
        compiler_params=_cparams(1),
        name="merge",
    )(x2, y_a, y_b, y_c, p, b_merge.reshape(1, 3 * d), w_branch_bf, w_out_bf,
      final_w.reshape(1, d))


def _rope_tables(s):
    half = RET_HEAD // 2
    inv_freq = 1.0 / (ROPE_BASE ** np.linspace(0.0, 1.0, half))
    ang = np.arange(s)[:, None] * inv_freq
    return jnp.asarray(np.cos(ang), F32), jnp.asarray(np.sin(ang), F32)


def _lora_block(decay_w2, iclr_a2):
    z = jnp.zeros((LORA, BRANCH_W), F32)
    return jnp.concatenate([jnp.concatenate([decay_w2, z], axis=1),
                            jnp.concatenate([z, iclr_a2], axis=1)], axis=0).astype(BF16)


def kernel(x, norm_w, w_in, b_merge, conv_w, conv_b, lru_gate_w, lru_gate_b, lru_lambda, shift_mu,
           decay_w0, decay_w2, iclr_a0, iclr_a2, k_k, k_a, r_k, lnx_w, lnx_b, w_branch, w_out,
           final_norm_w):
    bsz, s, d = x.shape
    depth = norm_w.shape[0]
    t = bsz * s
    tm = min(512, t)
    cos_t, sin_t = _rope_tables(s)
    w_branch_bf = w_branch.astype(BF16)
    w_out_bf = w_out.astype(BF16)
    x2 = x.reshape(t, d)
    for l in range(depth):
        w = w_in[l]
        w_main = jnp.concatenate([w[:, OFF_M:], w[:, OFF_C:OFF_M], w[:, OFF_B:OFF_C],
                                  jnp.zeros((d, P_PAD), F32)], axis=1).astype(BF16)
        p, y_a = _proj_lru(x2, norm_w[l], w_main, w[:, :OFF_B].astype(BF16), conv_w[l], conv_b[l],
                           lru_gate_w[l].astype(BF16), lru_gate_b[l], lru_lambda[l], s)
        p3 = p.reshape(bsz, s, P_COLS)
        y_b = _rwkv_branch(p3, shift_mu[l], _lora_block(decay_w2[l], iclr_a2[l]), decay_w0[l],
                           iclr_a0[l], k_k[l], k_a[l], r_k[l].reshape(-1), lnx_w[l], lnx_b[l])
        y_c = _ret_branch(p3, cos_t, sin_t)
        x2 = _merge(x2, y_a, y_b, y_c, p, b_merge[l], w_branch_bf, w_out_bf, l, final_norm_w,
                    l == depth - 1, tm)
    return x2.reshape(bsz, s, d)
```

```python
import functools
import math

import jax
import jax.numpy as jnp
import numpy as np
from jax import lax
from jax.experimental import pallas as pl
from jax.experimental.pallas import tpu as pltpu

F32 = jnp.float32
BF16 = jnp.bfloat16

D_MODEL = 1024
EPS = 1e-6
BRANCH_W = 1024
LRU_BLOCKS = 4
LRU_BLOCK = BRANCH_W // LRU_BLOCKS
CONV_W = 4
LRU_C = 8.0
RWKV_HEAD = 64
LORA = 64
RWKV_GN_EPS = 64e-5
RET_HEADS = 4
RET_HEAD = BRANCH_W // RET_HEADS
RET_CHUNK = 128
ROPE_BASE = 10000.0
RET_GN_EPS = 1e-5
B_COLS = 4 * BRANCH_W + 2 * LORA
OFF_B = 2 * BRANCH_W
OFF_C = OFF_B + B_COLS
OFF_M = OFF_C + 4 * BRANCH_W
N_IN = OFF_M + 3 * D_MODEL
LANES = 128
SUBLANES = 8
MXU_DIM = 256
RWKV_CHUNK = 64
RWKV_TB = 128
PROJ_TM = 512
LRU_PIECE_ROWS = 64
P_OFF_M = 0
P_OFF_C = P_OFF_M + 3 * D_MODEL
P_OFF_B = P_OFF_C + 4 * BRANCH_W
PROJ_TN = -(-(P_OFF_B + B_COLS) // (LRU_BLOCKS * MXU_DIM)) * MXU_DIM
P_COLS = PROJ_TN * LRU_BLOCKS
P_PAD = P_COLS - (P_OFF_B + B_COLS)
VMEM_LIMIT = 48 * 1024 * 1024
MERGE_TM = 512
MERGE_VMEM_LIMIT = 56 * 1024 * 1024


def _cparams(n_axes):
    return pltpu.CompilerParams(dimension_semantics=("arbitrary",) * n_axes,
                                vmem_limit_bytes=VMEM_LIMIT)


def _sigmoid(x):
    return 1.0 / (1.0 + jnp.exp(-x))


def _softplus(x):
    return jnp.maximum(x, 0.0) + jnp.log(1.0 + jnp.exp(-jnp.abs(x)))


def _dot(a, b):
    return jnp.dot(a, b, preferred_element_type=F32)


def _dot_nt(a, b):
    return lax.dot_general(a, b, (((1,), (1,)), ((), ())), preferred_element_type=F32)


def _bf(x):
    return x.astype(BF16)


def _split_hi_lo(x):
    hi = x.astype(BF16)
    lo = (x - hi.astype(F32)).astype(BF16)
    return hi, lo


def _lru_pieces(xg_ref, n, cw, cb, gw, gb, lam, tail, h, first, rows_per_piece, carry_out):
    tm, w = xg_ref.shape[2], xg_ref.shape[3]
    tail = jnp.where(first, 0.0, tail)
    h = jnp.where(first, 0.0, h)
    decay_rate = _softplus(-lam)
    sub = lax.broadcasted_iota(jnp.int32, (SUBLANES, w), 0)
    row = lax.broadcasted_iota(jnp.int32, (rows_per_piece, w), 0)
    for piece in range(tm // rows_per_piece):
        rows = slice(piece * rows_per_piece, (piece + 1) * rows_per_piece)
        x = xg_ref[0, n, rows, :]
        ga = xg_ref[1, n, rows, :]
        xfull = jnp.concatenate([tail, x], axis=0)
        tail = x[rows_per_piece - SUBLANES:, :]
        conv = cb + x * cw[CONV_W - 1:CONV_W, :]
        for s in range(1, CONV_W):
            shifted = pltpu.roll(xfull, s, 0)[SUBLANES:, :]
            conv = conv + shifted * cw[CONV_W - 1 - s:CONV_W - s, :]
        conv_bf = _bf(conv)
        r = _sigmoid(_dot(conv_bf, gw[0]) + gb[0:1, :])
        ig = _sigmoid(_dot(conv_bf, gw[1]) + gb[1:2, :])
        log_a = (-LRU_C) * r * decay_rate
        a = jnp.exp(log_a)
        z = -jnp.tanh(log_a) * (a * a + 1.0)
        mult = jnp.where(z > 0.0, z * lax.rsqrt(z), 0.0)
        if piece == 0:
            mult = jnp.where((row == 0) & first, 1.0, mult)
        b = mult * ig * conv

        tiles = []
        for g in range(rows_per_piece // SUBLANES):
            a8 = a[g * SUBLANES:(g + 1) * SUBLANES, :]
            b8 = b[g * SUBLANES:(g + 1) * SUBLANES, :]
            d = 1
            while d < SUBLANES:
                keep = sub >= d
                a_prev = jnp.where(keep, pltpu.roll(a8, d, 0), 1.0)
                b_prev = jnp.where(keep, pltpu.roll(b8, d, 0), 0.0)
                b8 = a8 * b_prev + b8
                a8 = a8 * a_prev
                d *= 2
            hs = a8 * h + b8
            tiles.append(hs)
            h = hs[SUBLANES - 1:SUBLANES, :]
        carry_out[:] = [tail, h]
        yield rows, jnp.concatenate(tiles, axis=0) * (ga * _sigmoid(ga))


def _proj_lru_kernel(x_ref, nw_ref, w_ref, wl_ref, cw_ref, cb_ref, gw_ref, gb_ref, lam_ref,
                     p_ref, ya_ref, h_ref, xg_ref, tail_ref, hc_ref, *, blocks_per_seq):
    i = pl.program_id(0)
    n = pl.program_id(1)

    @pl.when(n == 0)
    def _():
        x = x_ref[...]
        ms = jnp.mean(x * x, axis=-1, keepdims=True)
        h = _bf(x * lax.rsqrt(ms + EPS) * nw_ref[...])
        h_ref[...] = h
        xg = _dot(h, wl_ref[...])
        for half in range(2):
            for blk in range(LRU_BLOCKS):
                c0 = half * BRANCH_W + blk * LRU_BLOCK
                xg_ref[half, blk] = xg[:, c0:c0 + LRU_BLOCK]

    first = (i % blocks_per_seq) == 0
    carry = []
    pieces = _lru_pieces(xg_ref, n, cw_ref[n], cb_ref[n], gw_ref[n], gb_ref[n], lam_ref[n],
                         tail_ref[n], hc_ref[n], first, LRU_PIECE_ROWS, carry)
    h = h_ref[...]
    for c in range(p_ref.shape[1] // MXU_DIM):
        cols = slice(c * MXU_DIM, (c + 1) * MXU_DIM)
        p_ref[:, cols] = _dot(h, w_ref[:, cols]).astype(p_ref.dtype)
        piece = next(pieces, None)
        if piece is not None:
            rows, y = piece
            ya_ref[n, rows, :] = y.astype(ya_ref.dtype)
    assert next(pieces, None) is None
    tail_ref[n], hc_ref[n] = carry


def _proj_lru(x2, norm_w, w_main_bf, w_lru_bf, conv_w, conv_b, gate_w_bf, gate_b, lam, s):
    t, d = x2.shape
    tm = min(PROJ_TM, s)
    nblk, blk = LRU_BLOCKS, LRU_BLOCK
    per_block = lambda a: a.reshape(a.shape[0], nblk, blk).transpose(1, 0, 2)
    par = lambda shp: pl.BlockSpec(shp, lambda i, j: (0,) * len(shp))
    return pl.pallas_call(
        functools.partial(_proj_lru_kernel, blocks_per_seq=s // tm),
        out_shape=(jax.ShapeDtypeStruct((t, P_COLS), BF16),
                   jax.ShapeDtypeStruct((nblk, t, blk), BF16)),
        grid=(t // tm, nblk),
        in_specs=[pl.BlockSpec((tm, d), lambda i, j: (i, 0)),
                  par((1, d)),
                  pl.BlockSpec((d, PROJ_TN), lambda i, j: (0, j)),
                  par((d, 2 * BRANCH_W)),
                  par((nblk, CONV_W, blk)), par((nblk, 1, blk)), par((nblk, 2, blk, blk)),
                  par((nblk, 2, blk)), par((nblk, 1, blk))],
        out_specs=(pl.BlockSpec((tm, PROJ_TN), lambda i, j: (i, j)),
                   pl.BlockSpec((nblk, tm, blk), lambda i, j: (0, i, 0))),
        scratch_shapes=[pltpu.VMEM((tm, d), BF16),
                        pltpu.VMEM((2, nblk, tm, blk), F32),
                        pltpu.VMEM((nblk, SUBLANES, blk), F32),
                        pltpu.VMEM((nblk, 1, blk), F32)],
        compiler_params=_cparams(2),
        name="proj_rglru",
    )(x2, norm_w.reshape(1, d), w_main_bf, w_lru_bf, per_block(conv_w),
      per_block(conv_b.reshape(1, -1)), gate_w_bf.transpose(1, 0, 2, 3), per_block(gate_b),
      per_block(lam.reshape(1, -1)))


def _seg64_sum(x, bd_ones):
    xb = _bf(x)
    out = [_dot(xb[:, j * MXU_DIM:(j + 1) * MXU_DIM], bd_ones) for j in range(x.shape[1] // MXU_DIM)]
    return jnp.concatenate(out, axis=1)


def _shift_rows(x, prev_row):
    rolled = pltpu.roll(x, 1, 0)
    head = rolled[0:SUBLANES, :]
    row = lax.broadcasted_iota(jnp.int32, head.shape, 0)
    head = jnp.where(row == 0, jnp.broadcast_to(prev_row, head.shape), head)
    return jnp.concatenate([head, rolled[SUBLANES:, :]], axis=0)


def _rwkv_kernel(r_ref, k_ref, v_ref, g_ref, lo_ref, mu_ref, mulo_ref, w2a_ref, w0_ref, a0_ref,
                 kk_ref, ka_ref, rk_ref, lnw_ref, lnb_ref, o_ref, prev_ref, prevlo_ref, state_ref):
    nbat, tb, w = r_ref.shape
    L = RWKV_CHUNK
    G = MXU_DIM
    n_chunks = tb // L
    n_groups = w // G
    hshift = RWKV_HEAD.bit_length() - 1

    @pl.when(pl.program_id(0) == 0)
    def _():
        prev_ref[...] = jnp.zeros_like(prev_ref)
        prevlo_ref[...] = jnp.zeros_like(prevlo_ref)
        state_ref[...] = jnp.zeros_like(state_ref)

    rg = lax.broadcasted_iota(jnp.int32, (G, G), 0)
    cg = lax.broadcasted_iota(jnp.int32, (G, G), 1)
    same_head = (rg >> hshift) == (cg >> hshift)
    bd_ones = same_head.astype(BF16)
    r2 = lax.broadcasted_iota(jnp.int32, (2 * L, G), 0)
    c2 = lax.broadcasted_iota(jnp.int32, (2 * L, G), 1) & (L - 1)
    score_mask = (c2 < jnp.where(r2 >= L, r2 - (L - 1), r2)).astype(BF16)
    re = lax.broadcasted_iota(jnp.int32, (L, G), 0)
    ce = lax.broadcasted_iota(jnp.int32, (L, G), 1) & (L - 1)
    strict_w = ce < re
    eye_w = (ce == re).astype(F32)
    rt = lax.broadcasted_iota(jnp.int32, (tb, tb), 0)
    ct = lax.broadcasted_iota(jnp.int32, (tb, tb), 1)
    tril_chunks = ((ct <= rt) & ((ct >> hshift) == (rt >> hshift))).astype(BF16)
    tril2 = jnp.concatenate([tril_chunks, tril_chunks], axis=1)
    assert L == RWKV_HEAD
    heads_per_tile = LANES // RWKV_HEAD
    lane_t = lax.broadcasted_iota(jnp.int32, (L, LANES), 1) >> hshift
    head_pick = [(lane_t == hh).astype(BF16) for hh in range(heads_per_tile)]
    tile_bd = bd_ones[0:LANES, 0:LANES]
    zero_lt = jnp.zeros((L, LANES), BF16)

    def bd(x):
        xb = _bf(x)
        cols = []
        for lt in range(G // LANES):
            tile = xb[:, lt * LANES:(lt + 1) * LANES]
            blocks = [zero_lt] * (G // L)
            for hh in range(heads_per_tile):
                blocks[lt * heads_per_tile + hh] = tile * head_pick[hh]
            cols.append(jnp.concatenate(blocks, axis=0))
        return jnp.concatenate(cols, axis=1)

    def token_shift(x, prev_row, mu_row):
        return x + (_shift_rows(x, prev_row) - x) * mu_row

    units = []
    post = []
    for bi in range(nbat):
        r_raw, k_raw, v_raw, g_raw, lo_raw = (ref[bi].astype(F32)
                                              for ref in (r_ref, k_ref, v_ref, g_ref, lo_ref))
        r = token_shift(r_raw, prev_ref[bi, 0:1, :], mu_ref[0:1, :])
        k = token_shift(k_raw, prev_ref[bi, 1:2, :], mu_ref[1:2, :])
        v = token_shift(v_raw, prev_ref[bi, 2:3, :], mu_ref[2:3, :])
        g = token_shift(g_raw, prev_ref[bi, 3:4, :], mu_ref[3:4, :])
        lo = token_shift(lo_raw, prevlo_ref[bi], mulo_ref[...])
        prev_ref[bi, 0:1, :] = r_raw[tb - 1:tb, :]
        prev_ref[bi, 1:2, :] = k_raw[tb - 1:tb, :]
        prev_ref[bi, 2:3, :] = v_raw[tb - 1:tb, :]
        prev_ref[bi, 3:4, :] = g_raw[tb - 1:tb, :]
        prevlo_ref[bi] = lo_raw[tb - 1:tb, :]

        lane_lo = lax.broadcasted_iota(jnp.int32, lo.shape, 1)
        lo_in = _bf(jnp.where(lane_lo < LORA, jnp.tanh(lo), lo))
        dwa = _dot(lo_in, w2a_ref[...])
        ld = (-math.exp(-0.5)) * _sigmoid(w0_ref[...] + dwa[:, :w])
        a = _sigmoid(a0_ref[...] + dwa[:, w:])

        kk = k * kk_ref[...]
        kk = kk * lax.rsqrt(jnp.maximum(_seg64_sum(kk * kk, bd_ones), 1e-24))
        kv = k * (1.0 + (a - 1.0) * ka_ref[...])
        b = kk * a

        cum = _dot(tril2, jnp.concatenate(_split_hi_lo(ld), axis=0))
        p_last = [jnp.exp(cum[(c + 1) * L - 1:(c + 1) * L, :]) for c in range(n_chunks)]
        p_inc = jnp.exp(cum)
        p_exc = jnp.exp(cum - ld)
        p_inv = jnp.exp(-cum)
        p_hat = p_inv * jnp.concatenate([jnp.broadcast_to(pl_, (L, w)) for pl_ in p_last], axis=0)

        kap_t = kk * p_exc
        r_t = r * p_inc
        b_t = b * p_inv
        k_t = kv * p_inv
        b_h = b * p_hat
        k_h = kv * p_hat
        post.append((r, kv, v, g))
        for c in range(n_chunks):
            rows = slice(c * L, (c + 1) * L)
            for gi in range(n_groups):
                cols = slice(gi * G, (gi + 1) * G)
                units.append(dict(
                    c=c, bi=bi, gi=gi,
                    gq=_bf(jnp.concatenate([kap_t[rows, cols], r_t[rows, cols]], axis=0)),
                    b_t=b_t[rows, cols], k_t=k_t[rows, cols], v=v[rows, cols],
                    zh=jnp.concatenate([b_h[rows, cols], k_h[rows, cols]], axis=0),
                    p_last=p_last[c][:, cols]))

    for u in units:
        scb = _dot_nt(u["gq"], bd(u["b_t"]))
        sck = _dot_nt(u["gq"], bd(u["k_t"]))
        u["pw"] = jnp.where(strict_w, -scb[0:L], 0.0)
        u["r_b"] = _bf(scb[L:2 * L]) * score_mask[L:2 * L]
        u["ak_rk"] = _bf(sck) * score_mask
        u["tinv"] = eye_w + u["pw"]
    for u in units:
        akrk = _dot(u["ak_rk"], bd(u["v"]))
        u["akv"] = akrk[0:L]
        u["rkv"] = akrk[L:2 * L]
        u["zh_t"] = _bf(u["zh"].T)
        u["p_col"] = jnp.broadcast_to(u["p_last"], (LANES, G)).T
    for u in units:
        u["pw_next"] = _dot(_bf(u["pw"]), bd(u["pw"]))
    for _ in range(int(math.log2(L)) - 2):
        for u in units:
            u["pw"] = u["pw_next"]
            both = _dot(_bf(jnp.concatenate([u["tinv"], u["pw"]], axis=0)), bd(u["pw"]))
            u["tinv"] = u["tinv"] + both[0:L]
            u["pw_next"] = both[L:2 * L]
    for u in units:
        u["tinv"] = u["tinv"] + _dot(_bf(u["tinv"]), bd(u["pw_next"]))

    ys = {}
    n_q = G // LANES
    zero_q = jnp.zeros((LANES, LANES), BF16)
    for c in range(n_chunks):
        cu = [u for u in units if u["c"] == c]
        for u in cu:
            st = [state_ref[u["bi"], u["gi"], q] for q in range(n_q)]
            u["st"] = st
            st_bd = jnp.concatenate(
                [jnp.concatenate([_bf(st[q]) * tile_bd if qq == q else zero_q for qq in range(n_q)],
                                 axis=1) for q in range(n_q)], axis=0)
            x0 = _dot(u["gq"], st_bd)
            u["rhs"] = -(x0[0:L] + u["akv"])
            u["y"] = x0[L:2 * L] + u["rkv"]
        for u in cu:
            u["u"] = _dot(_bf(u["tinv"]), bd(u["rhs"]))
        for u in cu:
            ys[(u["bi"], c, u["gi"])] = u["y"] + _dot(u["r_b"], bd(u["u"]))
            uv = _bf(jnp.concatenate([u["u"], u["v"]], axis=0))
            for q in range(n_q):
                qs = slice(q * LANES, (q + 1) * LANES)
                state_ref[u["bi"], u["gi"], q] = (u["st"][q] * u["p_col"][qs, :]
                                                  + _dot(u["zh_t"][qs, :], uv[:, qs]))

    for bi in range(nbat):
        r, kv, v, g = post[bi]
        y = jnp.concatenate(
            [jnp.concatenate([ys[(bi, c, gi)] for gi in range(n_groups)], axis=1)
             for c in range(n_chunks)], axis=0)
        mean = _seg64_sum(y, bd_ones) * (1.0 / RWKV_HEAD)
        d = y - mean
        var = _seg64_sum(d * d, bd_ones) * (1.0 / RWKV_HEAD)
        yn = d * lax.rsqrt(var + RWKV_GN_EPS) * lnw_ref[...] + lnb_ref[...]
        bonus = _seg64_sum(r * kv * rk_ref[...], bd_ones) * v
        o_ref[bi] = ((yn + bonus) * (g * _sigmoid(g))).astype(o_ref.dtype)


def _rwkv_branch(p3, mu, w2a_bf, w0, a0, k_k, k_a, r_k, lnx_w, lnx_b):
    bsz, s, _ = p3.shape
    w = BRANCH_W
    tb = min(RWKV_TB, s)
    nb = s // tb
    row = lambda x: x.reshape(1, -1)
    mu4 = mu[:4 * w].reshape(4, w)
    mulo = mu[4 * w:].reshape(1, 2 * LORA)
    tok = lambda c: pl.BlockSpec((bsz, tb, w), lambda i, c=c: (0, i, P_OFF_B // w + c))
    par = lambda shp: pl.BlockSpec(shp, lambda i: (0,) * len(shp))
    out = pl.pallas_call(
        _rwkv_kernel,
        out_shape=jax.ShapeDtypeStruct((bsz, s, w), BF16),
        grid=(nb,),
        in_specs=[tok(0), tok(1), tok(2), tok(3),
                  pl.BlockSpec((bsz, tb, 2 * LORA), lambda i: (0, i, (P_OFF_B + 4 * w) // (2 * LORA))),
                  par((4, w)), par((1, 2 * LORA)), par((2 * LORA, 2 * w)),
                  par((1, w)), par((1, w)), par((1, w)), par((1, w)), par((1, w)),
                  par((1, w)), par((1, w))],
        out_specs=pl.BlockSpec((bsz, tb, w), lambda i: (0, i, 0)),
        scratch_shapes=[pltpu.VMEM((bsz, 4, w), F32), pltpu.VMEM((bsz, 1, 2 * LORA), F32),
                        pltpu.VMEM((bsz, w // MXU_DIM, MXU_DIM // LANES, LANES, LANES), F32)],
        compiler_params=_cparams(1),
        name="rwkv7",
    )(p3, p3, p3, p3, p3, mu4, mulo, w2a_bf, row(w0), row(a0), row(k_k), row(k_a),
      row(r_k), row(lnx_w), row(lnx_b))
    return out.reshape(bsz * s, w)


RET_LOG_G = [math.log1p(-(2.0 ** (-5.0 - h))) for h in range(RET_HEADS)]


def _ret_tables(mask_ref, qdec_ref, kdec_ref):
    C, dh = RET_CHUNK, RET_HEAD
    ri = lax.broadcasted_iota(jnp.int32, (C, C), 0)
    ci = lax.broadcasted_iota(jnp.int32, (C, C), 1)
    rel = (ri - ci).astype(F32)
    rowf = lax.broadcasted_iota(jnp.int32, (C, dh), 0).astype(F32)
    for h in range(RET_HEADS):
        mask_ref[h] = jnp.where(rel >= 0.0, jnp.exp(RET_LOG_G[h] * jnp.maximum(rel, 0.0)),
                                0.0) * (dh ** -0.5)
        qdec_ref[h] = jnp.exp(RET_LOG_G[h] * (rowf + 1.0))
        kdec_ref[h] = jnp.exp(RET_LOG_G[h] * (C - 1.0 - rowf)) * (dh ** -0.5)


def _ret_pieces(q_ref, k_ref, v_ref, g_ref, cos_ref, sin_ref, yc_ref, state_ref, mask_ref,
                qdec_ref, kdec_ref, first):
    C, dh = RET_CHUNK, RET_HEAD
    half = dh // 2
    for c in range(q_ref.shape[0] // C):
        rows = slice(c * C, (c + 1) * C)
        cos = cos_ref[rows, :]
        sin = sin_ref[rows, :]

        def rot(t):
            t1, t2 = t[:, :half], t[:, half:]
            return jnp.concatenate([t1 * cos - t2 * sin, t1 * sin + t2 * cos], axis=1)

        for h in range(RET_HEADS):
            sl = slice(h * dh, (h + 1) * dh)
            q = rot(q_ref[rows, sl].astype(F32))
            k = rot(k_ref[rows, sl].astype(F32))
            v_bf = v_ref[rows, sl]
            scores = _dot_nt(_bf(q), _bf(k)) * mask_ref[h]
            inner = _dot(_bf(scores), v_bf)
            st = state_ref[h]
            if c == 0:
                st = jnp.where(first, 0.0, st)
            cross = _dot(_bf(q * qdec_ref[h]), _bf(st))
            state_ref[h] = st * math.exp(RET_LOG_G[h] * C) + _dot(_bf((k * kdec_ref[h]).T), v_bf)
            out = inner + cross
            mu = jnp.mean(out, axis=-1, keepdims=True)
            dlt = out - mu
            var = jnp.mean(dlt * dlt, axis=-1, keepdims=True)
            g = g_ref[rows, sl].astype(F32)
            yc_ref[rows, sl] = (dlt * lax.rsqrt(var + RET_GN_EPS) * (g * _sigmoid(g))).astype(
                yc_ref.dtype)
            yield


def _merge_kernel(x_ref, ya_ref, yb_ref, q_ref, k_ref, v_ref, g_ref, cos_ref, sin_ref, pm_ref,
                  bm_ref, wb_ref, wo_ref, fw_ref, o_ref, yc_ref, macc_ref, state_ref, mask_ref,
                  qdec_ref, kdec_ref, *, final_norm, blocks_per_seq):
    i = pl.program_id(0)
    d = x_ref.shape[1]
    n_col = d // MXU_DIM

    @pl.when(i == 0)
    def _():
        state_ref[...] = jnp.zeros_like(state_ref)
        _ret_tables(mask_ref, qdec_ref, kdec_ref)

    def gate(n, cols):
        off = n * d
        return _sigmoid(pm_ref[:, off + cols.start:off + cols.stop].astype(F32)
                        + bm_ref[:, off + cols.start:off + cols.stop])

    pieces = _ret_pieces(q_ref, k_ref, v_ref, g_ref, cos_ref, sin_ref, yc_ref, state_ref,
                         mask_ref, qdec_ref, kdec_ref, (i % blocks_per_seq) == 0)
    n_pieces = (x_ref.shape[0] // RET_CHUNK) * RET_HEADS
    per_slot = -(-n_pieces // (2 * n_col))

    def run_pieces():
        for _ in range(per_slot):
            next(pieces, None)

    for cc in range(n_col):
        cols = slice(cc * MXU_DIM, (cc + 1) * MXU_DIM)
        pa = sum(_dot(ya_ref[blk], wb_ref[0, blk * LRU_BLOCK:(blk + 1) * LRU_BLOCK, cols])
                 for blk in range(LRU_BLOCKS))
        macc_ref[:, cols] = gate(0, cols) * pa
        run_pieces()
        macc_ref[:, cols] += gate(1, cols) * _dot(yb_ref[...], wb_ref[1, :, cols])
        run_pieces()
    assert next(pieces, None) is None

    yc = yc_ref[...]
    for cc in range(n_col):
        cols = slice(cc * MXU_DIM, (cc + 1) * MXU_DIM)
        macc_ref[:, cols] += gate(2, cols) * _dot(yc, wb_ref[2, :, cols])
    merged = _bf(macc_ref[...])
    for cc in range(n_col):
        cols = slice(cc * MXU_DIM, (cc + 1) * MXU_DIM)
        o_ref[:, cols] = x_ref[:, cols] + _dot(merged, wo_ref[:, cols])
    if final_norm:
        out = o_ref[...]
        ms = jnp.mean(out * out, axis=-1, keepdims=True)
        o_ref[...] = out * lax.rsqrt(ms + EPS) * fw_ref[...]


def _merge(x2, y_a, y_b, p, cos_t, sin_t, b_merge, w_branch_bf, w_out_bf, layer, final_w,
           final_norm, s):
    t, d = x2.shape
    w = BRANCH_W
    tm = min(MERGE_TM, s)
    blocks_per_seq = s // tm
    rows = lambda n: pl.BlockSpec((tm, n), lambda i: (i, 0))
    tok = lambda c: pl.BlockSpec((tm, w), lambda i, c=c: (i, P_OFF_C // w + c))
    tab = pl.BlockSpec((tm, RET_HEAD // 2), lambda i: (i % blocks_per_seq, 0))
    return pl.pallas_call(
        functools.partial(_merge_kernel, final_norm=final_norm, blocks_per_seq=blocks_per_seq),
        out_shape=jax.ShapeDtypeStruct((t, d), F32),
        grid=(t // tm,),
        in_specs=[rows(d), pl.BlockSpec((LRU_BLOCKS, tm, LRU_BLOCK), lambda i: (0, i, 0)),
                  rows(w), tok(0), tok(1), tok(2), tok(3), tab, tab,
                  pl.BlockSpec((tm, 3 * d), lambda i: (i, P_OFF_M // (3 * d))),
                  pl.BlockSpec((1, 3 * d), lambda i: (0, 0)),
                  pl.BlockSpec((None, 3, w, d), lambda i: (layer, 0, 0, 0)),
                  pl.BlockSpec((None, d, d), lambda i: (layer, 0, 0)),
                  pl.BlockSpec((1, d), lambda i: (0, 0))],
        out_specs=rows(d),
        scratch_shapes=[pltpu.VMEM((tm, w), BF16), pltpu.VMEM((tm, d), F32),
                        pltpu.VMEM((RET_HEADS, RET_HEAD, RET_HEAD), F32),
                        pltpu.VMEM((RET_HEADS, RET_CHUNK, RET_CHUNK), F32),
                        pltpu.VMEM((RET_HEADS, RET_CHUNK, RET_HEAD), F32),
                        pltpu.VMEM((RET_HEADS, RET_CHUNK, RET_HEAD), F32)],
        compiler_params=pltpu.CompilerParams(dimension_semantics=("arbitrary",),
                                             vmem_limit_bytes=MERGE_VMEM_LIMIT),
        name="merge_retention",
    )(x2, y_a, y_b, p, p, p, p, cos_t, sin_t, p, b_merge.reshape(1, 3 * d), w_branch_bf,
      w_out_bf, final_w.reshape(1, d))


def _rope_tables(s):
    half = RET_HEAD // 2
    inv_freq = 1.0 / (ROPE_BASE ** np.linspace(0.0, 1.0, half))
    ang = np.arange(s)[:, None] * inv_freq
    return jnp.asarray(np.cos(ang), F32), jnp.asarray(np.sin(ang), F32)


def _lora_block(decay_w2, iclr_a2):
    z = jnp.zeros((LORA, BRANCH_W), F32)
    return jnp.concatenate([jnp.concatenate([decay_w2, z], axis=1),
                            jnp.concatenate([z, iclr_a2], axis=1)], axis=0).astype(BF16)


def kernel(x, norm_w, w_in, b_merge, conv_w, conv_b, lru_gate_w, lru_gate_b, lru_lambda, shift_mu,
           decay_w0, decay_w2, iclr_a0, iclr_a2, k_k, k_a, r_k, lnx_w, lnx_b, w_branch, w_out,
           final_norm_w):
    bsz, s, d = x.shape
    depth = norm_w.shape[0]
    t = bsz * s
    cos_t, sin_t = _rope_tables(s)
    w_branch_bf = w_branch.astype(BF16)
    w_out_bf = w_out.astype(BF16)
    x2 = x.reshape(t, d)
    for l in range(depth):
        w = w_in[l]
        w_main = jnp.concatenate([w[:, OFF_M:], w[:, OFF_C:OFF_M], w[:, OFF_B:OFF_C],
                                  jnp.zeros((d, P_PAD), F32)], axis=1).astype(BF16)
        p, y_a = _proj_lru(x2, norm_w[l], w_main, w[:, :OFF_B].astype(BF16), conv_w[l], conv_b[l],
                           lru_gate_w[l].astype(BF16), lru_gate_b[l], lru_lambda[l], s)
        p3 = p.reshape(bsz, s, P_COLS)
        y_b = _rwkv_branch(p3, shift_mu[l], _lora_block(decay_w2[l], iclr_a2[l]), decay_w0[l],
                           iclr_a0[l], k_k[l], k_a[l], r_k[l].reshape(-1), lnx_w[l], lnx_b[l])
        x2 = _merge(x2, y_a, y_b, p, cos_t, sin_t, b_merge[l], w_branch_bf, w_out_bf, l,
                    final_norm_w, l == depth - 1, s)
    return x2.reshape(bsz, s, d)
```

```python
import functools
import math

import jax
import jax.numpy as jnp
import numpy as np
from jax import lax
from jax.experimental import pallas as pl
from jax.experimental.pallas import tpu as pltpu

F32 = jnp.float32
BF16 = jnp.bfloat16

D_MODEL = 1024
EPS = 1e-6
BRANCH_W = 1024
LRU_BLOCKS = 4
LRU_BLOCK = BRANCH_W // LRU_BLOCKS
CONV_W = 4
LRU_C = 8.0
RWKV_HEAD = 64
LORA = 64
RWKV_GN_EPS = 64e-5
RET_HEADS = 4
RET_HEAD = BRANCH_W // RET_HEADS
RET_CHUNK = 128
ROPE_BASE = 10000.0
RET_GN_EPS = 1e-5
B_COLS = 4 * BRANCH_W + 2 * LORA
OFF_B = 2 * BRANCH_W
OFF_C = OFF_B + B_COLS
OFF_M = OFF_C + 4 * BRANCH_W
N_IN = OFF_M + 3 * D_MODEL
LANES = 128
SUBLANES = 8
MXU_DIM = 256
RWKV_CHUNK = 64
RWKV_TB = 128
PROJ_TM = 512
LRU_PIECE_ROWS = 64
PROJ_TN = 3 * D_MODEL
P_COLS = PROJ_TN * LRU_BLOCKS
P_OFF_B = 0
P_OFF_C = OFF_C - OFF_B
P_OFF_M = P_COLS - PROJ_TN
assert OFF_B + (LRU_BLOCKS - 1) * PROJ_TN >= OFF_M and N_IN - OFF_M == PROJ_TN
VMEM_LIMIT = 48 * 1024 * 1024
MERGE_TM = 512
MERGE_VMEM_LIMIT = 56 * 1024 * 1024


def _cparams(n_axes):
    return pltpu.CompilerParams(dimension_semantics=("arbitrary",) * n_axes,
                                vmem_limit_bytes=VMEM_LIMIT)


def _sigmoid(x):
    return 1.0 / (1.0 + jnp.exp(-x))


def _softplus(x):
    return jnp.maximum(x, 0.0) + jnp.log(1.0 + jnp.exp(-jnp.abs(x)))


def _dot(a, b):
    return jnp.dot(a, b, preferred_element_type=F32)


def _dot_nt(a, b):
    return lax.dot_general(a, b, (((1,), (1,)), ((), ())), preferred_element_type=F32)


def _bf(x):
    return x.astype(BF16)


def _split_hi_lo(x):
    hi = x.astype(BF16)
    lo = (x - hi.astype(F32)).astype(BF16)
    return hi, lo


def _lru_pieces(xg_ref, n, cw, cb, gw, gb, lam, tail, h, first, rows_per_piece, carry_out):
    tm, w = xg_ref.shape[2], xg_ref.shape[3]
    tail = jnp.where(first, 0.0, tail)
    h = jnp.where(first, 0.0, h)
    decay_rate = _softplus(-lam)
    sub = lax.broadcasted_iota(jnp.int32, (SUBLANES, w), 0)
    row = lax.broadcasted_iota(jnp.int32, (rows_per_piece, w), 0)
    for piece in range(tm // rows_per_piece):
        rows = slice(piece * rows_per_piece, (piece + 1) * rows_per_piece)
        x = xg_ref[0, n, rows, :]
        ga = xg_ref[1, n, rows, :]
        xfull = jnp.concatenate([tail, x], axis=0)
        tail = x[rows_per_piece - SUBLANES:, :]
        conv = cb + x * cw[CONV_W - 1:CONV_W, :]
        for s in range(1, CONV_W):
            shifted = pltpu.roll(xfull, s, 0)[SUBLANES:, :]
            conv = conv + shifted * cw[CONV_W - 1 - s:CONV_W - s, :]
        conv_bf = _bf(conv)
        r = _sigmoid(_dot(conv_bf, gw[0]) + gb[0:1, :])
        ig = _sigmoid(_dot(conv_bf, gw[1]) + gb[1:2, :])
        log_a = (-LRU_C) * r * decay_rate
        a = jnp.exp(log_a)
        z = -jnp.tanh(log_a) * (a * a + 1.0)
        mult = jnp.where(z > 0.0, z * lax.rsqrt(z), 0.0)
        if piece == 0:
            mult = jnp.where((row == 0) & first, 1.0, mult)
        b = mult * ig * conv

        tiles = []
        for g in range(rows_per_piece // SUBLANES):
            a8 = a[g * SUBLANES:(g + 1) * SUBLANES, :]
            b8 = b[g * SUBLANES:(g + 1) * SUBLANES, :]
            d = 1
            while d < SUBLANES:
                keep = sub >= d
                a_prev = jnp.where(keep, pltpu.roll(a8, d, 0), 1.0)
                b_prev = jnp.where(keep, pltpu.roll(b8, d, 0), 0.0)
                b8 = a8 * b_prev + b8
                a8 = a8 * a_prev
                d *= 2
            hs = a8 * h + b8
            tiles.append(hs)
            h = hs[SUBLANES - 1:SUBLANES, :]
        carry_out[:] = [tail, h]
        yield rows, jnp.concatenate(tiles, axis=0) * (ga * _sigmoid(ga))


def _proj_lru_kernel(x_ref, nw_ref, w_ref, wl_ref, cw_ref, cb_ref, gw_ref, gb_ref, lam_ref,
                     p_ref, ya_ref, h_ref, xg_ref, tail_ref, hc_ref, *, blocks_per_seq):
    i = pl.program_id(0)
    n = pl.program_id(1)

    @pl.when(n == 0)
    def _():
        x = x_ref[...]
        ms = jnp.mean(x * x, axis=-1, keepdims=True)
        h = _bf(x * lax.rsqrt(ms + EPS) * nw_ref[...])
        h_ref[...] = h
        xg = _dot(h, wl_ref[...])
        for half in range(2):
            for blk in range(LRU_BLOCKS):
                c0 = half * BRANCH_W + blk * LRU_BLOCK
                xg_ref[half, blk] = xg[:, c0:c0 + LRU_BLOCK]

    first = (i % blocks_per_seq) == 0
    carry = []
    pieces = _lru_pieces(xg_ref, n, cw_ref[n], cb_ref[n], gw_ref[n], gb_ref[n], lam_ref[n],
                         tail_ref[n], hc_ref[n], first, LRU_PIECE_ROWS, carry)
    h = h_ref[...]
    for c in range(p_ref.shape[1] // MXU_DIM):
        cols = slice(c * MXU_DIM, (c + 1) * MXU_DIM)
        p_ref[:, cols] = _dot(h, w_ref[0, :, cols]).astype(p_ref.dtype)
        piece = next(pieces, None)
        if piece is not None:
            rows, y = piece
            ya_ref[n, rows, :] = y.astype(ya_ref.dtype)
    assert next(pieces, None) is None
    tail_ref[n], hc_ref[n] = carry


def _proj_lru(x2, norm_w, w_in_bf, layer, conv_w, conv_b, gate_w_bf, gate_b, lam, s):
    t, d = x2.shape
    overshoot = (OFF_B + (LRU_BLOCKS - 1) * PROJ_TN - (N_IN - PROJ_TN)) // LANES
    main_off = lambda j: (OFF_B // LANES + j * (PROJ_TN // LANES)
                          - overshoot * (j // (LRU_BLOCKS - 1))) * LANES
    tm = min(PROJ_TM, s)
    nblk, blk = LRU_BLOCKS, LRU_BLOCK
    per_block = lambda a: a.reshape(a.shape[0], nblk, blk).transpose(1, 0, 2)
    par = lambda shp: pl.BlockSpec(shp, lambda i, j: (0,) * len(shp))
    return pl.pallas_call(
        functools.partial(_proj_lru_kernel, blocks_per_seq=s // tm),
        out_shape=(jax.ShapeDtypeStruct((t, P_COLS), BF16),
                   jax.ShapeDtypeStruct((nblk, t, blk), BF16)),
        grid=(t // tm, nblk),
        in_specs=[pl.BlockSpec((tm, d), lambda i, j: (i, 0)),
                  par((1, d)),
                  pl.BlockSpec((pl.Element(1), pl.Element(d), pl.Element(PROJ_TN)),
                               lambda i, j: (layer, 0, main_off(j))),
                  pl.BlockSpec((None, d, OFF_B), lambda i, j: (layer, 0, 0)),
                  par((nblk, CONV_W, blk)), par((nblk, 1, blk)), par((nblk, 2, blk, blk)),
                  par((nblk, 2, blk)), par((nblk, 1, blk))],
        out_specs=(pl.BlockSpec((tm, PROJ_TN), lambda i, j: (i, j)),
                   pl.BlockSpec((nblk, tm, blk), lambda i, j: (0, i, 0))),
        scratch_shapes=[pltpu.VMEM((tm, d), BF16),
                        pltpu.VMEM((2, nblk, tm, blk), F32),
                        pltpu.VMEM((nblk, SUBLANES, blk), F32),
                        pltpu.VMEM((nblk, 1, blk), F32)],
        compiler_params=_cparams(2),
        name="proj_rglru",
    )(x2, norm_w.reshape(1, d), w_in_bf, w_in_bf, per_block(conv_w),
      per_block(conv_b.reshape(1, -1)), gate_w_bf.transpose(1, 0, 2, 3), per_block(gate_b),
      per_block(lam.reshape(1, -1)))


def _seg64_sum(x, bd_ones):
    xb = _bf(x)
    out = [_dot(xb[:, j * MXU_DIM:(j + 1) * MXU_DIM], bd_ones) for j in range(x.shape[1] // MXU_DIM)]
    return jnp.concatenate(out, axis=1)


def _shift_rows(x, prev_row):
    rolled = pltpu.roll(x, 1, 0)
    head = rolled[0:SUBLANES, :]
    row = lax.broadcasted_iota(jnp.int32, head.shape, 0)
    head = jnp.where(row == 0, jnp.broadcast_to(prev_row, head.shape), head)
    return jnp.concatenate([head, rolled[SUBLANES:, :]], axis=0)


def _rwkv_kernel(r_ref, k_ref, v_ref, g_ref, lo_ref, mu_ref, mulo_ref, w2a_ref, w0_ref, a0_ref,
                 kk_ref, ka_ref, rk_ref, lnw_ref, lnb_ref, o_ref, prev_ref, prevlo_ref, state_ref):
    nbat, tb, w = r_ref.shape
    L = RWKV_CHUNK
    G = MXU_DIM
    n_chunks = tb // L
    n_groups = w // G
    hshift = RWKV_HEAD.bit_length() - 1

    @pl.when(pl.program_id(0) == 0)
    def _():
        prev_ref[...] = jnp.zeros_like(prev_ref)
        prevlo_ref[...] = jnp.zeros_like(prevlo_ref)
        state_ref[...] = jnp.zeros_like(state_ref)

    rg = lax.broadcasted_iota(jnp.int32, (G, G), 0)
    cg = lax.broadcasted_iota(jnp.int32, (G, G), 1)
    same_head = (rg >> hshift) == (cg >> hshift)
    bd_ones = same_head.astype(BF16)
    r2 = lax.broadcasted_iota(jnp.int32, (2 * L, G), 0)
    c2 = lax.broadcasted_iota(jnp.int32, (2 * L, G), 1) & (L - 1)
    score_mask = (c2 < jnp.where(r2 >= L, r2 - (L - 1), r2)).astype(BF16)
    re = lax.broadcasted_iota(jnp.int32, (L, G), 0)
    ce = lax.broadcasted_iota(jnp.int32, (L, G), 1) & (L - 1)
    strict_w = ce < re
    eye_w = (ce == re).astype(F32)
    rt = lax.broadcasted_iota(jnp.int32, (tb, tb), 0)
    ct = lax.broadcasted_iota(jnp.int32, (tb, tb), 1)
    tril_chunks = ((ct <= rt) & ((ct >> hshift) == (rt >> hshift))).astype(BF16)
    tril2 = jnp.concatenate([tril_chunks, tril_chunks], axis=1)
    assert L == RWKV_HEAD
    heads_per_tile = LANES // RWKV_HEAD
    lane_t = lax.broadcasted_iota(jnp.int32, (L, LANES), 1) >> hshift
    head_pick = [(lane_t == hh).astype(BF16) for hh in range(heads_per_tile)]
    tile_bd = bd_ones[0:LANES, 0:LANES]
    zero_lt = jnp.zeros((L, LANES), BF16)

    def bd(x):
        xb = _bf(x)
        cols = []
        for lt in range(G // LANES):
            tile = xb[:, lt * LANES:(lt + 1) * LANES]
            blocks = [zero_lt] * (G // L)
            for hh in range(heads_per_tile):
                blocks[lt * heads_per_tile + hh] = tile * head_pick[hh]
            cols.append(jnp.concatenate(blocks, axis=0))
        return jnp.concatenate(cols, axis=1)

    def token_shift(x, prev_row, mu_row):
        return x + (_shift_rows(x, prev_row) - x) * mu_row

    units = []
    post = []
    for bi in range(nbat):
        r_raw, k_raw, v_raw, g_raw, lo_raw = (ref[bi].astype(F32)
                                              for ref in (r_ref, k_ref, v_ref, g_ref, lo_ref))
        r = token_shift(r_raw, prev_ref[bi, 0:1, :], mu_ref[0:1, :])
        k = token_shift(k_raw, prev_ref[bi, 1:2, :], mu_ref[1:2, :])
        v = token_shift(v_raw, prev_ref[bi, 2:3, :], mu_ref[2:3, :])
        g = token_shift(g_raw, prev_ref[bi, 3:4, :], mu_ref[3:4, :])
        lo = token_shift(lo_raw, prevlo_ref[bi], mulo_ref[...])
        prev_ref[bi, 0:1, :] = r_raw[tb - 1:tb, :]
        prev_ref[bi, 1:2, :] = k_raw[tb - 1:tb, :]
        prev_ref[bi, 2:3, :] = v_raw[tb - 1:tb, :]
        prev_ref[bi, 3:4, :] = g_raw[tb - 1:tb, :]
        prevlo_ref[bi] = lo_raw[tb - 1:tb, :]

        lane_lo = lax.broadcasted_iota(jnp.int32, lo.shape, 1)
        lo_in = _bf(jnp.where(lane_lo < LORA, jnp.tanh(lo), lo))
        dwa = _dot(lo_in, w2a_ref[...])
        ld = (-math.exp(-0.5)) * _sigmoid(w0_ref[...] + dwa[:, :w])
        a = _sigmoid(a0_ref[...] + dwa[:, w:])

        kk = k * kk_ref[...]
        kk = kk * lax.rsqrt(jnp.maximum(_seg64_sum(kk * kk, bd_ones), 1e-24))
        kv = k * (1.0 + (a - 1.0) * ka_ref[...])
        b = kk * a

        cum = _dot(tril2, jnp.concatenate(_split_hi_lo(ld), axis=0))
        p_last = [jnp.exp(cum[(c + 1) * L - 1:(c + 1) * L, :]) for c in range(n_chunks)]
        p_inc = jnp.exp(cum)
        p_exc = jnp.exp(cum - ld)
        p_inv = jnp.exp(-cum)
        p_hat = p_inv * jnp.concatenate([jnp.broadcast_to(pl_, (L, w)) for pl_ in p_last], axis=0)

        kap_t = kk * p_exc
        r_t = r * p_inc
        b_t = b * p_inv
        k_t = kv * p_inv
        b_h = b * p_hat
        k_h = kv * p_hat
        post.append((r, kv, v, g))
        for c in range(n_chunks):
            rows = slice(c * L, (c + 1) * L)
            for gi in range(n_groups):
                cols = slice(gi * G, (gi + 1) * G)
                units.append(dict(
                    c=c, bi=bi, gi=gi,
                    gq=_bf(jnp.concatenate([kap_t[rows, cols], r_t[rows, cols]], axis=0)),
                    b_t=b_t[rows, cols], k_t=k_t[rows, cols], v=v[rows, cols],
                    zh=jnp.concatenate([b_h[rows, cols], k_h[rows, cols]], axis=0),
                    p_last=p_last[c][:, cols]))

    for u in units:
        scb = _dot_nt(u["gq"], bd(u["b_t"]))
        sck = _dot_nt(u["gq"], bd(u["k_t"]))
        u["pw"] = jnp.where(strict_w, -scb[0:L], 0.0)
        u["r_b"] = _bf(scb[L:2 * L]) * score_mask[L:2 * L]
        u["ak_rk"] = _bf(sck) * score_mask
        u["tinv"] = eye_w + u["pw"]
    for u in units:
        akrk = _dot(u["ak_rk"], bd(u["v"]))
        u["akv"] = akrk[0:L]
        u["rkv"] = akrk[L:2 * L]
        u["zh_t"] = _bf(u["zh"].T)
        u["p_col"] = jnp.broadcast_to(u["p_last"], (LANES, G)).T
    for u in units:
        u["pw_next"] = _dot(_bf(u["pw"]), bd(u["pw"]))
    for _ in range(int(math.log2(L)) - 2):
        for u in units:
            u["pw"] = u["pw_next"]
            both = _dot(_bf(jnp.concatenate([u["tinv"], u["pw"]], axis=0)), bd(u["pw"]))
            u["tinv"] = u["tinv"] + both[0:L]
            u["pw_next"] = both[L:2 * L]
    for u in units:
        u["tinv"] = u["tinv"] + _dot(_bf(u["tinv"]), bd(u["pw_next"]))

    ys = {}
    n_q = G // LANES
    zero_q = jnp.zeros((LANES, LANES), BF16)
    for c in range(n_chunks):
        cu = [u for u in units if u["c"] == c]
        for u in cu:
            st = [state_ref[u["bi"], u["gi"], q] for q in range(n_q)]
            u["st"] = st
            st_bd = jnp.concatenate(
                [jnp.concatenate([_bf(st[q]) * tile_bd if qq == q else zero_q for qq in range(n_q)],
                                 axis=1) for q in range(n_q)], axis=0)
            x0 = _dot(u["gq"], st_bd)
            u["rhs"] = -(x0[0:L] + u["akv"])
            u["y"] = x0[L:2 * L] + u["rkv"]
        for u in cu:
            u["u"] = _dot(_bf(u["tinv"]), bd(u["rhs"]))
        for u in cu:
            ys[(u["bi"], c, u["gi"])] = u["y"] + _dot(u["r_b"], bd(u["u"]))
            uv = _bf(jnp.concatenate([u["u"], u["v"]], axis=0))
            for q in range(n_q):
                qs = slice(q * LANES, (q + 1) * LANES)
                state_ref[u["bi"], u["gi"], q] = (u["st"][q] * u["p_col"][qs, :]
                                                  + _dot(u["zh_t"][qs, :], uv[:, qs]))

    for bi in range(nbat):
        r, kv, v, g = post[bi]
        y = jnp.concatenate(
            [jnp.concatenate([ys[(bi, c, gi)] for gi in range(n_groups)], axis=1)
             for c in range(n_chunks)], axis=0)
        mean = _seg64_sum(y, bd_ones) * (1.0 / RWKV_HEAD)
        d = y - mean
        var = _seg64_sum(d * d, bd_ones) * (1.0 / RWKV_HEAD)
        yn = d * lax.rsqrt(var + RWKV_GN_EPS) * lnw_ref[...] + lnb_ref[...]
        bonus = _seg64_sum(r * kv * rk_ref[...], bd_ones) * v
        o_ref[bi] = ((yn + bonus) * (g * _sigmoid(g))).astype(o_ref.dtype)


def _rwkv_branch(p3, mu, w2a_bf, w0, a0, k_k, k_a, r_k, lnx_w, lnx_b):
    bsz, s, _ = p3.shape
    w = BRANCH_W
    tb = min(RWKV_TB, s)
    nb = s // tb
    row = lambda x: x.reshape(1, -1)
    mu4 = mu[:4 * w].reshape(4, w)
    mulo = mu[4 * w:].reshape(1, 2 * LORA)
    tok = lambda c: pl.BlockSpec((bsz, tb, w), lambda i, c=c: (0, i, P_OFF_B // w + c))
    par = lambda shp: pl.BlockSpec(shp, lambda i: (0,) * len(shp))
    out = pl.pallas_call(
        _rwkv_kernel,
        out_shape=jax.ShapeDtypeStruct((bsz, s, w), BF16),
        grid=(nb,),
        in_specs=[tok(0), tok(1), tok(2), tok(3),
                  pl.BlockSpec((bsz, tb, 2 * LORA), lambda i: (0, i, (P_OFF_B + 4 * w) // (2 * LORA))),
                  par((4, w)), par((1, 2 * LORA)), par((2 * LORA, 2 * w)),
                  par((1, w)), par((1, w)), par((1, w)), par((1, w)), par((1, w)),
                  par((1, w)), par((1, w))],
        out_specs=pl.BlockSpec((bsz, tb, w), lambda i: (0, i, 0)),
        scratch_shapes=[pltpu.VMEM((bsz, 4, w), F32), pltpu.VMEM((bsz, 1, 2 * LORA), F32),
                        pltpu.VMEM((bsz, w // MXU_DIM, MXU_DIM // LANES, LANES, LANES), F32)],
        compiler_params=_cparams(1),
        name="rwkv7",
    )(p3, p3, p3, p3, p3, mu4, mulo, w2a_bf, row(w0), row(a0), row(k_k), row(k_a),
      row(r_k), row(lnx_w), row(lnx_b))
    return out.reshape(bsz * s, w)


RET_LOG_G = [math.log1p(-(2.0 ** (-5.0 - h))) for h in range(RET_HEADS)]


def _ret_tables(mask_ref, qdec_ref, kdec_ref):
    C, dh = RET_CHUNK, RET_HEAD
    ri = lax.broadcasted_iota(jnp.int32, (C, C), 0)
    ci = lax.broadcasted_iota(jnp.int32, (C, C), 1)
    rel = (ri - ci).astype(F32)
    rowf = lax.broadcasted_iota(jnp.int32, (C, dh), 0).astype(F32)
    for h in range(RET_HEADS):
        mask_ref[h] = jnp.where(rel >= 0.0, jnp.exp(RET_LOG_G[h] * jnp.maximum(rel, 0.0)),
                                0.0) * (dh ** -0.5)
        qdec_ref[h] = jnp.exp(RET_LOG_G[h] * (rowf + 1.0))
        kdec_ref[h] = jnp.exp(RET_LOG_G[h] * (C - 1.0 - rowf)) * (dh ** -0.5)


def _ret_pieces(q_ref, k_ref, v_ref, g_ref, cos_ref, sin_ref, yc_ref, state_ref, mask_ref,
                qdec_ref, kdec_ref, first):
    C, dh = RET_CHUNK, RET_HEAD
    half = dh // 2
    for c in range(q_ref.shape[0] // C):
        rows = slice(c * C, (c + 1) * C)
        cos = cos_ref[rows, :]
        sin = sin_ref[rows, :]

        def rot(t):
            t1, t2 = t[:, :half], t[:, half:]
            return jnp.concatenate([t1 * cos - t2 * sin, t1 * sin + t2 * cos], axis=1)

        for h in range(RET_HEADS):
            sl = slice(h * dh, (h + 1) * dh)
            q = rot(q_ref[rows, sl].astype(F32))
            k = rot(k_ref[rows, sl].astype(F32))
            v_bf = v_ref[rows, sl]
            scores = _dot_nt(_bf(q), _bf(k)) * mask_ref[h]
            inner = _dot(_bf(scores), v_bf)
            st = state_ref[h]
            if c == 0:
                st = jnp.where(first, 0.0, st)
            cross = _dot(_bf(q * qdec_ref[h]), _bf(st))
            state_ref[h] = st * math.exp(RET_LOG_G[h] * C) + _dot(_bf((k * kdec_ref[h]).T), v_bf)
            out = inner + cross
            mu = jnp.mean(out, axis=-1, keepdims=True)
            dlt = out - mu
            var = jnp.mean(dlt * dlt, axis=-1, keepdims=True)
            g = g_ref[rows, sl].astype(F32)
            yc_ref[rows, sl] = (dlt * lax.rsqrt(var + RET_GN_EPS) * (g * _sigmoid(g))).astype(
                yc_ref.dtype)
            yield


def _merge_kernel(x_ref, ya_ref, yb_ref, q_ref, k_ref, v_ref, g_ref, cos_ref, sin_ref, pm_ref,
                  bm_ref, wb_ref, wo_ref, fw_ref, o_ref, yc_ref, macc_ref, state_ref, mask_ref,
                  qdec_ref, kdec_ref, *, final_norm, blocks_per_seq):
    i = pl.program_id(0)
    d = x_ref.shape[1]
    n_col = d // MXU_DIM

    @pl.when(i == 0)
    def _():
        state_ref[...] = jnp.zeros_like(state_ref)
        _ret_tables(mask_ref, qdec_ref, kdec_ref)

    def gate(n, cols):
        off = n * d
        return _sigmoid(pm_ref[:, off + cols.start:off + cols.stop].astype(F32)
                        + bm_ref[:, off + cols.start:off + cols.stop])

    pieces = _ret_pieces(q_ref, k_ref, v_ref, g_ref, cos_ref, sin_ref, yc_ref, state_ref,
                         mask_ref, qdec_ref, kdec_ref, (i % blocks_per_seq) == 0)
    n_pieces = (x_ref.shape[0] // RET_CHUNK) * RET_HEADS
    per_slot = -(-n_pieces // (2 * n_col))

    def run_pieces():
        for _ in range(per_slot):
            next(pieces, None)

    for cc in range(n_col):
        cols = slice(cc * MXU_DIM, (cc + 1) * MXU_DIM)
        pa = sum(_dot(ya_ref[blk], wb_ref[0, blk * LRU_BLOCK:(blk + 1) * LRU_BLOCK, cols])
                 for blk in range(LRU_BLOCKS))
        macc_ref[:, cols] = gate(0, cols) * pa
        run_pieces()
        macc_ref[:, cols] += gate(1, cols) * _dot(yb_ref[...], wb_ref[1, :, cols])
        run_pieces()
    assert next(pieces, None) is None

    yc = yc_ref[...]
    for cc in range(n_col):
        cols = slice(cc * MXU_DIM, (cc + 1) * MXU_DIM)
        macc_ref[:, cols] += gate(2, cols) * _dot(yc, wb_ref[2, :, cols])
    merged = _bf(macc_ref[...])
    for cc in range(n_col):
        cols = slice(cc * MXU_DIM, (cc + 1) * MXU_DIM)
        o_ref[:, cols] = x_ref[:, cols] + _dot(merged, wo_ref[:, cols])
    if final_norm:
        out = o_ref[...]
        ms = jnp.mean(out * out, axis=-1, keepdims=True)
        o_ref[...] = out * lax.rsqrt(ms + EPS) * fw_ref[...]


def _merge(x2, y_a, y_b, p, cos_t, sin_t, b_merge, w_branch_bf, w_out_bf, layer, final_w,
           final_norm, s):
    t, d = x2.shape
    w = BRANCH_W
    tm = min(MERGE_TM, s)
    blocks_per_seq = s // tm
    rows = lambda n: pl.BlockSpec((tm, n), lambda i: (i, 0))
    tok = lambda c: pl.BlockSpec((pl.Element(tm), pl.Element(w)),
                                 lambda i, c=c: (i * tm, P_OFF_C + c * w))
    tab =pl.BlockSpec((tm, RET_HEAD // 2), lambda i: (i % blocks_per_seq, 0))
    return pl.pallas_call(
        functools.partial(_merge_kernel, final_norm=final_norm, blocks_per_seq=blocks_per_seq),
        out_shape=jax.ShapeDtypeStruct((t, d), F32),
        grid=(t // tm,),
        in_specs=[rows(d), pl.BlockSpec((LRU_BLOCKS, tm, LRU_BLOCK), lambda i: (0, i, 0)),
                  rows(w), tok(0), tok(1), tok(2), tok(3), tab, tab,
                  pl.BlockSpec((tm, 3 * d), lambda i: (i, P_OFF_M // (3 * d))),
                  pl.BlockSpec((1, 3 * d), lambda i: (0, 0)),
                  pl.BlockSpec((None, 3, w, d), lambda i: (layer, 0, 0, 0)),
                  pl.BlockSpec((None, d, d), lambda i: (layer, 0, 0)),
                  pl.BlockSpec((1, d), lambda i: (0, 0))],
        out_specs=rows(d),
        scratch_shapes=[pltpu.VMEM((tm, w), BF16), pltpu.VMEM((tm, d), F32),
                        pltpu.VMEM((RET_HEADS, RET_HEAD, RET_HEAD), F32),
                        pltpu.VMEM((RET_HEADS, RET_CHUNK, RET_CHUNK), F32),
                        pltpu.VMEM((RET_HEADS, RET_CHUNK, RET_HEAD), F32),
                        pltpu.VMEM((RET_HEADS, RET_CHUNK, RET_HEAD), F32)],
        compiler_params=pltpu.CompilerParams(dimension_semantics=("arbitrary",),
                                             vmem_limit_bytes=MERGE_VMEM_LIMIT),
        name="merge_retention",
    )(x2, y_a, y_b, p, p, p, p, cos_t, sin_t, p, b_merge.reshape(1, 3 * d), w_branch_bf,
      w_out_bf, final_w.reshape(1, d))


def _rope_tables(s):
    half = RET_HEAD // 2
    inv_freq = 1.0 / (ROPE_BASE ** np.linspace(0.0, 1.0, half))
    ang = np.arange(s)[:, None] * inv_freq
    return jnp.asarray(np.cos(ang), F32), jnp.asarray(np.sin(ang), F32)


def _lora_block(decay_w2, iclr_a2):
    z = jnp.zeros((LORA, BRANCH_W), F32)
    return jnp.concatenate([jnp.concatenate([decay_w2, z], axis=1),
                            jnp.concatenate([z, iclr_a2], axis=1)], axis=0).astype(BF16)


def kernel(x, norm_w, w_in, b_merge, conv_w, conv_b, lru_gate_w, lru_gate_b, lru_lambda, shift_mu,
           decay_w0, decay_w2, iclr_a0, iclr_a2, k_k, k_a, r_k, lnx_w, lnx_b, w_branch, w_out,
           final_norm_w):
    bsz, s, d = x.shape
    depth = norm_w.shape[0]
    t = bsz * s
    cos_t, sin_t = _rope_tables(s)
    w_in_bf = w_in.astype(BF16)
    w_branch_bf = w_branch.astype(BF16)
    w_out_bf = w_out.astype(BF16)
    x2 = x.reshape(t, d)
    for l in range(depth):
        p, y_a = _proj_lru(x2, norm_w[l], w_in_bf, l, conv_w[l], conv_b[l],
                           lru_gate_w[l].astype(BF16), lru_gate_b[l], lru_lambda[l], s)
        p3 = p.reshape(bsz, s, P_COLS)
        y_b = _rwkv_branch(p3, shift_mu[l], _lora_block(decay_w2[l], iclr_a2[l]), decay_w0[l],
                           iclr_a0[l], k_k[l], k_a[l], r_k[l].reshape(-1), lnx_w[l], lnx_b[l])
        x2 = _merge(x2, y_a, y_b, p, cos_t, sin_t, b_merge[l], w_branch_bf, w_out_bf, l,
                    final_norm_w, l == depth - 1, s)
    return x2.reshape(bsz, s, d)
```

```python
import functools
import math

import jax
import jax.numpy as jnp
import numpy as np
from jax import lax
from jax.experimental import pallas as pl
from jax.experimental.pallas import tpu as pltpu

F32 = jnp.float32
BF16 = jnp.bfloat16

D_MODEL = 1024
EPS = 1e-6
BRANCH_W = 1024
LRU_BLOCKS = 4
LRU_BLOCK = BRANCH_W // LRU_BLOCKS
CONV_W = 4
LRU_C = 8.0
RWKV_HEAD = 64
LORA = 64
RWKV_GN_EPS = 64e-5
RET_HEADS = 4
RET_HEAD = BRANCH_W // RET_HEADS
RET_CHUNK = 128
ROPE_BASE = 10000.0
RET_GN_EPS = 1e-5
B_COLS = 4 * BRANCH_W + 2 * LORA
OFF_B = 2 * BRANCH_W
OFF_C = OFF_B + B_COLS
OFF_M = OFF_C + 4 * BRANCH_W
N_IN = OFF_M + 3 * D_MODEL
LANES = 128
SUBLANES = 8
MXU_DIM = 256
RWKV_CHUNK = 64
RWKV_TB = 256
PROJ_TM = 512
LRU_PIECE_ROWS = 64
PROJ_TN = 3 * D_MODEL
P_COLS = PROJ_TN * LRU_BLOCKS
P_OFF_B = 0
P_OFF_C = OFF_C - OFF_B
P_OFF_M = P_COLS - PROJ_TN
assert OFF_B + (LRU_BLOCKS - 1) * PROJ_TN >= OFF_M and N_IN - OFF_M == PROJ_TN
VMEM_LIMIT = 48 * 1024 * 1024
MERGE_TM = 512
MERGE_VMEM_LIMIT = 56 * 1024 * 1024


def _cparams(n_axes):
    return pltpu.CompilerParams(dimension_semantics=("arbitrary",) * n_axes,
                                vmem_limit_bytes=VMEM_LIMIT)


def _sigmoid(x):
    return 1.0 / (1.0 + jnp.exp(-x))


def _softplus(x):
    return jnp.maximum(x, 0.0) + jnp.log(1.0 + jnp.exp(-jnp.abs(x)))


def _dot(a, b):
    return jnp.dot(a, b, preferred_element_type=F32)


def _dot_nt(a, b):
    return lax.dot_general(a, b, (((1,), (1,)), ((), ())), preferred_element_type=F32)


def _bf(x):
    return x.astype(BF16)


def _split_hi_lo(x):
    hi = x.astype(BF16)
    lo = (x - hi.astype(F32)).astype(BF16)
    return hi, lo


def _lru_pieces(xg_ref, n, cw, cb, gw, gb, lam, tail, h, first, rows_per_piece, carry_out):
    tm, w = xg_ref.shape[2], xg_ref.shape[3]
    tail = jnp.where(first, 0.0, tail)
    h = jnp.where(first, 0.0, h)
    decay_rate = _softplus(-lam)
    sub = lax.broadcasted_iota(jnp.int32, (SUBLANES, w), 0)
    row = lax.broadcasted_iota(jnp.int32, (rows_per_piece, w), 0)
    for piece in range(tm // rows_per_piece):
        rows = slice(piece * rows_per_piece, (piece + 1) * rows_per_piece)
        x = xg_ref[0, n, rows, :]
        ga = xg_ref[1, n, rows, :]
        xfull = jnp.concatenate([tail, x], axis=0)
        tail = x[rows_per_piece - SUBLANES:, :]
        conv = cb + x * cw[CONV_W - 1:CONV_W, :]
        for s in range(1, CONV_W):
            shifted = pltpu.roll(xfull, s, 0)[SUBLANES:, :]
            conv = conv + shifted * cw[CONV_W - 1 - s:CONV_W - s, :]
        conv_bf = _bf(conv)
        r = _sigmoid(_dot(conv_bf, gw[0]) + gb[0:1, :])
        ig = _sigmoid(_dot(conv_bf, gw[1]) + gb[1:2, :])
        log_a = (-LRU_C) * r * decay_rate
        a = jnp.exp(log_a)
        z = -jnp.tanh(log_a) * (a * a + 1.0)
        mult = jnp.where(z > 0.0, z * lax.rsqrt(z), 0.0)
        if piece == 0:
            mult = jnp.where((row == 0) & first, 1.0, mult)
        b = mult * ig * conv

        tiles = []
        for g in range(rows_per_piece // SUBLANES):
            a8 = a[g * SUBLANES:(g + 1) * SUBLANES, :]
            b8 = b[g * SUBLANES:(g + 1) * SUBLANES, :]
            d = 1
            while d < SUBLANES:
                keep = sub >= d
                a_prev = jnp.where(keep, pltpu.roll(a8, d, 0), 1.0)
                b_prev = jnp.where(keep, pltpu.roll(b8, d, 0), 0.0)
                b8 = a8 * b_prev + b8
                a8 = a8 * a_prev
                d *= 2
            hs = a8 * h + b8
            tiles.append(hs)
            h = hs[SUBLANES - 1:SUBLANES, :]
        carry_out[:] = [tail, h]
        yield rows, jnp.concatenate(tiles, axis=0) * (ga * _sigmoid(ga))


def _proj_lru_kernel(x_ref, nw_ref, w_ref, wl_ref, cw_ref, cb_ref, gw_ref, gb_ref, lam_ref,
                     p_ref, ya_ref, h_ref, xg_ref, tail_ref, hc_ref, *, blocks_per_seq):
    i = pl.program_id(0)
    n = pl.program_id(1)

    @pl.when(n == 0)
    def _():
        x = x_ref[...]
        ms = jnp.mean(x * x, axis=-1, keepdims=True)
        h = _bf(x * lax.rsqrt(ms + EPS) * nw_ref[...])
        h_ref[...] = h
        xg = _dot(h, wl_ref[...])
        for half in range(2):
            for blk in range(LRU_BLOCKS):
                c0 = half * BRANCH_W + blk * LRU_BLOCK
                xg_ref[half, blk] = xg[:, c0:c0 + LRU_BLOCK]

    first = (i % blocks_per_seq) == 0
    carry = []
    pieces = _lru_pieces(xg_ref, n, cw_ref[n], cb_ref[n], gw_ref[n], gb_ref[n], lam_ref[n],
                         tail_ref[n], hc_ref[n], first, LRU_PIECE_ROWS, carry)
    h = h_ref[...]
    n_chunks = p_ref.shape[1] // MXU_DIM
    per_chunk = -(-(x_ref.shape[0] // LRU_PIECE_ROWS) // n_chunks)
    for c in range(n_chunks):
        cols = slice(c * MXU_DIM, (c + 1) * MXU_DIM)
        p_ref[:, cols] = _dot(h, w_ref[0, :, cols]).astype(p_ref.dtype)
        for _ in range(per_chunk):
            piece = next(pieces, None)
            if piece is not None:
                rows, y = piece
                ya_ref[n, rows, :] = y.astype(ya_ref.dtype)
    assert next(pieces, None) is None
    tail_ref[n], hc_ref[n] = carry


def _proj_lru(x2, norm_w, w_in_bf, layer, conv_w, conv_b, gate_w_bf, gate_b, lam, s):
    t, d = x2.shape
    overshoot = (OFF_B + (LRU_BLOCKS - 1) * PROJ_TN - (N_IN - PROJ_TN)) // LANES
    main_off = lambda j: (OFF_B // LANES + j * (PROJ_TN // LANES)
                          - overshoot * (j // (LRU_BLOCKS - 1))) * LANES
    tm = min(PROJ_TM, s)
    nblk, blk = LRU_BLOCKS, LRU_BLOCK
    per_block = lambda a: a.reshape(a.shape[0], nblk, blk).transpose(1, 0, 2)
    par = lambda shp: pl.BlockSpec(shp, lambda i, j: (0,) * len(shp))
    return pl.pallas_call(
        functools.partial(_proj_lru_kernel, blocks_per_seq=s // tm),
        out_shape=(jax.ShapeDtypeStruct((t, P_COLS), BF16),
                   jax.ShapeDtypeStruct((nblk, t, blk), BF16)),
        grid=(t // tm, nblk),
        in_specs=[pl.BlockSpec((tm, d), lambda i, j: (i, 0)),
                  par((1, d)),
                  pl.BlockSpec((pl.Element(1), pl.Element(d), pl.Element(PROJ_TN)),
                               lambda i, j: (layer, 0, main_off(j))),
                  pl.BlockSpec((None, d, OFF_B), lambda i, j: (layer, 0, 0)),
                  par((nblk, CONV_W, blk)), par((nblk, 1, blk)), par((nblk, 2, blk, blk)),
                  par((nblk, 2, blk)), par((nblk, 1, blk))],
        out_specs=(pl.BlockSpec((tm, PROJ_TN), lambda i, j: (i, j)),
                   pl.BlockSpec((nblk, tm, blk), lambda i, j: (0, i, 0))),
        scratch_shapes=[pltpu.VMEM((tm, d), BF16),
                        pltpu.VMEM((2, nblk, tm, blk), F32),
                        pltpu.VMEM((nblk, SUBLANES, blk), F32),
                        pltpu.VMEM((nblk, 1, blk), F32)],
        compiler_params=_cparams(2),
        name="proj_rglru",
    )(x2, norm_w.reshape(1, d), w_in_bf, w_in_bf, per_block(conv_w),
      per_block(conv_b.reshape(1, -1)), gate_w_bf.transpose(1, 0, 2, 3), per_block(gate_b),
      per_block(lam.reshape(1, -1)))


def _seg64_sum(x, bd_ones):
    xb = _bf(x)
    out = [_dot(xb[:, j * MXU_DIM:(j + 1) * MXU_DIM], bd_ones) for j in range(x.shape[1] // MXU_DIM)]
    return jnp.concatenate(out, axis=1)


def _shift_rows(x, prev_row):
    rolled = pltpu.roll(x, 1, 0)
    head = rolled[0:SUBLANES, :]
    row = lax.broadcasted_iota(jnp.int32, head.shape, 0)
    head = jnp.where(row == 0, jnp.broadcast_to(prev_row, head.shape), head)
    return jnp.concatenate([head, rolled[SUBLANES:, :]], axis=0)


def _rwkv_kernel(r_ref, k_ref, v_ref, g_ref, lo_ref, mu_ref, mulo_ref, w2a_ref, w0_ref, a0_ref,
                 kk_ref, ka_ref, rk_ref, lnw_ref, lnb_ref, o_ref, prev_ref, prevlo_ref, state_ref):
    nbat, tb, w = r_ref.shape
    L = RWKV_CHUNK
    G = MXU_DIM
    n_chunks = tb // L
    n_groups = w // G
    hshift = RWKV_HEAD.bit_length() - 1

    @pl.when(pl.program_id(0) == 0)
    def _():
        prev_ref[...] = jnp.zeros_like(prev_ref)
        prevlo_ref[...] = jnp.zeros_like(prevlo_ref)
        state_ref[...] = jnp.zeros_like(state_ref)

    rg = lax.broadcasted_iota(jnp.int32, (G, G), 0)
    cg = lax.broadcasted_iota(jnp.int32, (G, G), 1)
    same_head = (rg >> hshift) == (cg >> hshift)
    bd_ones = same_head.astype(BF16)
    r2 = lax.broadcasted_iota(jnp.int32, (2 * L, G), 0)
    c2 = lax.broadcasted_iota(jnp.int32, (2 * L, G), 1) & (L - 1)
    score_mask = (c2 < jnp.where(r2 >= L, r2 - (L - 1), r2)).astype(BF16)
    re = lax.broadcasted_iota(jnp.int32, (L, G), 0)
    ce = lax.broadcasted_iota(jnp.int32, (L, G), 1) & (L - 1)
    strict_w = ce < re
    eye_w = (ce == re).astype(F32)
    rt = lax.broadcasted_iota(jnp.int32, (tb, tb), 0)
    ct = lax.broadcasted_iota(jnp.int32, (tb, tb), 1)
    tril_chunks = ((ct <= rt) & ((ct >> hshift) == (rt >> hshift))).astype(BF16)
    tril2 = jnp.concatenate([tril_chunks, tril_chunks], axis=1)
    assert L == RWKV_HEAD
    heads_per_tile = LANES // RWKV_HEAD
    lane_t = lax.broadcasted_iota(jnp.int32, (L, LANES), 1) >> hshift
    head_pick = [(lane_t == hh).astype(BF16) for hh in range(heads_per_tile)]
    tile_bd = bd_ones[0:LANES, 0:LANES]
    zero_lt = jnp.zeros((L, LANES), BF16)

    def bd(x):
        xb = _bf(x)
        cols = []
        for lt in range(G // LANES):
            tile = xb[:, lt * LANES:(lt + 1) * LANES]
            blocks = [zero_lt] * (G // L)
            for hh in range(heads_per_tile):
                blocks[lt * heads_per_tile + hh] = tile * head_pick[hh]
            cols.append(jnp.concatenate(blocks, axis=0))
        return jnp.concatenate(cols, axis=1)

    def token_shift(x, prev_row, mu_row):
        return x + (_shift_rows(x, prev_row) - x) * mu_row

    units = []
    post = []
    for bi in range(nbat):
        r_raw, k_raw, v_raw, g_raw, lo_raw = (ref[bi].astype(F32)
                                              for ref in (r_ref, k_ref, v_ref, g_ref, lo_ref))
        r = token_shift(r_raw, prev_ref[bi, 0:1, :], mu_ref[0:1, :])
        k = token_shift(k_raw, prev_ref[bi, 1:2, :], mu_ref[1:2, :])
        v = token_shift(v_raw, prev_ref[bi, 2:3, :], mu_ref[2:3, :])
        g = token_shift(g_raw, prev_ref[bi, 3:4, :], mu_ref[3:4, :])
        lo = token_shift(lo_raw, prevlo_ref[bi], mulo_ref[...])
        prev_ref[bi, 0:1, :] = r_raw[tb - 1:tb, :]
        prev_ref[bi, 1:2, :] = k_raw[tb - 1:tb, :]
        prev_ref[bi, 2:3, :] = v_raw[tb - 1:tb, :]
        prev_ref[bi, 3:4, :] = g_raw[tb - 1:tb, :]
        prevlo_ref[bi] = lo_raw[tb - 1:tb, :]

        lane_lo = lax.broadcasted_iota(jnp.int32, lo.shape, 1)
        lo_in = _bf(jnp.where(lane_lo < LORA, jnp.tanh(lo), lo))
        dwa = _dot(lo_in, w2a_ref[...])
        ld = (-math.exp(-0.5)) * _sigmoid(w0_ref[...] + dwa[:, :w])
        a = _sigmoid(a0_ref[...] + dwa[:, w:])

        kk = k * kk_ref[...]
        kk = kk * lax.rsqrt(jnp.maximum(_seg64_sum(kk * kk, bd_ones), 1e-24))
        kv = k * (1.0 + (a - 1.0) * ka_ref[...])
        b = kk * a

        cum = _dot(tril2, jnp.concatenate(_split_hi_lo(ld), axis=0))
        p_last = [jnp.exp(cum[(c + 1) * L - 1:(c + 1) * L, :]) for c in range(n_chunks)]
        p_inc = jnp.exp(cum)
        p_exc = jnp.exp(cum - ld)
        p_inv = jnp.exp(-cum)
        p_hat = p_inv * jnp.concatenate([jnp.broadcast_to(pl_, (L, w)) for pl_ in p_last], axis=0)

        kap_t = kk * p_exc
        r_t = r * p_inc
        b_t = b * p_inv
        k_t = kv * p_inv
        b_h = b * p_hat
        k_h = kv * p_hat
        post.append((r, kv, v, g))
        for c in range(n_chunks):
            rows = slice(c * L, (c + 1) * L)
            for gi in range(n_groups):
                cols = slice(gi * G, (gi + 1) * G)
                units.append(dict(
                    c=c, bi=bi, gi=gi,
                    gq=_bf(jnp.concatenate([kap_t[rows, cols], r_t[rows, cols]], axis=0)),
                    b_t=b_t[rows, cols], k_t=k_t[rows, cols], v=v[rows, cols],
                    zh=jnp.concatenate([b_h[rows, cols], k_h[rows, cols]], axis=0),
                    p_last=p_last[c][:, cols]))

    for u in units:
        scb = _dot_nt(u["gq"], bd(u["b_t"]))
        sck = _dot_nt(u["gq"], bd(u["k_t"]))
        u["pw"] = jnp.where(strict_w, -scb[0:L], 0.0)
        u["r_b"] = _bf(scb[L:2 * L]) * score_mask[L:2 * L]
        u["ak_rk"] = _bf(sck) * score_mask
        u["tinv"] = eye_w + u["pw"]
    for u in units:
        akrk = _dot(u["ak_rk"], bd(u["v"]))
        u["akv"] = akrk[0:L]
        u["rkv"] = akrk[L:2 * L]
        u["zh_t"] = _bf(u["zh"].T)
        u["p_col"] = jnp.broadcast_to(u["p_last"], (LANES, G)).T
    for u in units:
        u["pw_next"] = _dot(_bf(u["pw"]), bd(u["pw"]))
    for _ in range(int(math.log2(L)) - 2):
        for u in units:
            u["pw"] = u["pw_next"]
            both = _dot(_bf(jnp.concatenate([u["tinv"], u["pw"]], axis=0)), bd(u["pw"]))
            u["tinv"] = u["tinv"] + both[0:L]
            u["pw_next"] = both[L:2 * L]
    for u in units:
        u["tinv"] = u["tinv"] + _dot(_bf(u["tinv"]), bd(u["pw_next"]))

    ys = {}
    n_q = G // LANES
    zero_q = jnp.zeros((LANES, LANES), BF16)
    for c in range(n_chunks):
        cu = [u for u in units if u["c"] == c]
        for u in cu:
            st = [state_ref[u["bi"], u["gi"], q] for q in range(n_q)]
            u["st"] = st
            st_bd = jnp.concatenate(
                [jnp.concatenate([_bf(st[q]) * tile_bd if qq == q else zero_q for qq in range(n_q)],
                                 axis=1) for q in range(n_q)], axis=0)
            x0 = _dot(u["gq"], st_bd)
            u["rhs"] = -(x0[0:L] + u["akv"])
            u["y"] = x0[L:2 * L] + u["rkv"]
        for u in cu:
            u["u"] = _dot(_bf(u["tinv"]), bd(u["rhs"]))
        for u in cu:
            ys[(u["bi"], c, u["gi"])] = u["y"] + _dot(u["r_b"], bd(u["u"]))
            uv = _bf(jnp.concatenate([u["u"], u["v"]], axis=0))
            for q in range(n_q):
                qs = slice(q * LANES, (q + 1) * LANES)
                state_ref[u["bi"], u["gi"], q] = (u["st"][q] * u["p_col"][qs, :]
                                                  + _dot(u["zh_t"][qs, :], uv[:, qs]))

    for bi in range(nbat):
        r, kv, v, g = post[bi]
        y = jnp.concatenate(
            [jnp.concatenate([ys[(bi, c, gi)] for gi in range(n_groups)], axis=1)
             for c in range(n_chunks)], axis=0)
        mean = _seg64_sum(y, bd_ones) * (1.0 / RWKV_HEAD)
        d = y - mean
        var = _seg64_sum(d * d, bd_ones) * (1.0 / RWKV_HEAD)
        yn = d * lax.rsqrt(var + RWKV_GN_EPS) * lnw_ref[...] + lnb_ref[...]
        bonus = _seg64_sum(r * kv * rk_ref[...], bd_ones) * v
        o_ref[bi] = ((yn + bonus) * (g * _sigmoid(g))).astype(o_ref.dtype)


def _rwkv_branch(p3, mu, w2a_bf, w0, a0, k_k, k_a, r_k, lnx_w, lnx_b):
    bsz, s, _ = p3.shape
    w = BRANCH_W
    tb = min(RWKV_TB, s)
    nb = s // tb
    row = lambda x: x.reshape(1, -1)
    mu4 = mu[:4 * w].reshape(4, w)
    mulo = mu[4 * w:].reshape(1, 2 * LORA)
    tok = lambda c: pl.BlockSpec((bsz, tb, w), lambda i, c=c: (0, i, P_OFF_B // w + c))
    par = lambda shp: pl.BlockSpec(shp, lambda i: (0,) * len(shp))
    out = pl.pallas_call(
        _rwkv_kernel,
        out_shape=jax.ShapeDtypeStruct((bsz, s, w), BF16),
        grid=(nb,),
        in_specs=[tok(0), tok(1), tok(2), tok(3),
                  pl.BlockSpec((bsz, tb, 2 * LORA), lambda i: (0, i, (P_OFF_B + 4 * w) // (2 * LORA))),
                  par((4, w)), par((1, 2 * LORA)), par((2 * LORA, 2 * w)),
                  par((1, w)), par((1, w)), par((1, w)), par((1, w)), par((1, w)),
                  par((1, w)), par((1, w))],
        out_specs=pl.BlockSpec((bsz, tb, w), lambda i: (0, i, 0)),
        scratch_shapes=[pltpu.VMEM((bsz, 4, w), F32), pltpu.VMEM((bsz, 1, 2 * LORA), F32),
                        pltpu.VMEM((bsz, w // MXU_DIM, MXU_DIM // LANES, LANES, LANES), F32)],
        compiler_params=_cparams(1),
        name="rwkv7",
    )(p3, p3, p3, p3, p3, mu4, mulo, w2a_bf, row(w0), row(a0), row(k_k), row(k_a),
      row(r_k), row(lnx_w), row(lnx_b))
    return out.reshape(bsz * s, w)


RET_LOG_G = [math.log1p(-(2.0 ** (-5.0 - h))) for h in range(RET_HEADS)]


def _ret_tables(mask_ref, qdec_ref, kdec_ref):
    C, dh = RET_CHUNK, RET_HEAD
    ri = lax.broadcasted_iota(jnp.int32, (C, C), 0)
    ci = lax.broadcasted_iota(jnp.int32, (C, C), 1)
    rel = (ri - ci).astype(F32)
    rowf = lax.broadcasted_iota(jnp.int32, (C, dh), 0).astype(F32)
    for h in range(RET_HEADS):
        mask_ref[h] = jnp.where(rel >= 0.0, jnp.exp(RET_LOG_G[h] * jnp.maximum(rel, 0.0)),
                                0.0) * (dh ** -0.5)
        qdec_ref[h] = jnp.exp(RET_LOG_G[h] * (rowf + 1.0))
        kdec_ref[h] = jnp.exp(RET_LOG_G[h] * (C - 1.0 - rowf)) * (dh ** -0.5)


def _ret_pieces(q_ref, k_ref, v_ref, g_ref, cos_ref, sin_ref, yc_ref, state_ref, mask_ref,
                qdec_ref, kdec_ref, first):
    C, dh = RET_CHUNK, RET_HEAD
    half = dh // 2
    for c in range(q_ref.shape[0] // C):
        rows = slice(c * C, (c + 1) * C)
        cos = cos_ref[rows, :]
        sin = sin_ref[rows, :]

        def rot(t):
            t1, t2 = t[:, :half], t[:, half:]
            return jnp.concatenate([t1 * cos - t2 * sin, t1 * sin + t2 * cos], axis=1)

        for h in range(RET_HEADS):
            sl = slice(h * dh, (h + 1) * dh)
            q = rot(q_ref[rows, sl].astype(F32))
            k = rot(k_ref[rows, sl].astype(F32))
            v_bf = v_ref[rows, sl]
            scores = _dot_nt(_bf(q), _bf(k)) * mask_ref[h]
            inner = _dot(_bf(scores), v_bf)
            st = state_ref[h]
            if c == 0:
                st = jnp.where(first, 0.0, st)
            cross = _dot(_bf(q * qdec_ref[h]), _bf(st))
            state_ref[h] = st * math.exp(RET_LOG_G[h] * C) + _dot(_bf((k * kdec_ref[h]).T), v_bf)
            out = inner + cross
            mu = jnp.mean(out, axis=-1, keepdims=True)
            dlt = out - mu
            var = jnp.mean(dlt * dlt, axis=-1, keepdims=True)
            g = g_ref[rows, sl].astype(F32)
            yc_ref[rows, sl] = (dlt * lax.rsqrt(var + RET_GN_EPS) * (g * _sigmoid(g))).astype(
                yc_ref.dtype)
            yield


def _merge_kernel(x_ref, ya_ref, yb_ref, q_ref, k_ref, v_ref, g_ref, cos_ref, sin_ref, pm_ref,
                  bm_ref, wb_ref, wo_ref, fw_ref, o_ref, yc_ref, macc_ref, state_ref, mask_ref,
                  qdec_ref, kdec_ref, *, final_norm, blocks_per_seq):
    i = pl.program_id(0)
    d = x_ref.shape[1]
    n_col = d // MXU_DIM

    @pl.when(i == 0)
    def _():
        state_ref[...] = jnp.zeros_like(state_ref)
        _ret_tables(mask_ref, qdec_ref, kdec_ref)

    def gate(n, cols):
        off = n * d
        return _sigmoid(pm_ref[:, off + cols.start:off + cols.stop].astype(F32)
                        + bm_ref[:, off + cols.start:off + cols.stop])

    pieces = _ret_pieces(q_ref, k_ref, v_ref, g_ref, cos_ref, sin_ref, yc_ref, state_ref,
                         mask_ref, qdec_ref, kdec_ref, (i % blocks_per_seq) == 0)
    n_pieces = (x_ref.shape[0] // RET_CHUNK) * RET_HEADS
    per_slot = -(-n_pieces // (2 * n_col))

    def run_pieces():
        for _ in range(per_slot):
            next(pieces, None)

    for cc in range(n_col):
        cols = slice(cc * MXU_DIM, (cc + 1) * MXU_DIM)
        pa = sum(_dot(ya_ref[blk], wb_ref[0, blk * LRU_BLOCK:(blk + 1) * LRU_BLOCK, cols])
                 for blk in range(LRU_BLOCKS))
        macc_ref[:, cols] = gate(0, cols) * pa
        run_pieces()
        macc_ref[:, cols] += gate(1, cols) * _dot(yb_ref[...], wb_ref[1, :, cols])
        run_pieces()
    assert next(pieces, None) is None

    yc = yc_ref[...]
    for cc in range(n_col):
        cols = slice(cc * MXU_DIM, (cc + 1) * MXU_DIM)
        macc_ref[:, cols] += gate(2, cols) * _dot(yc, wb_ref[2, :, cols])
    merged = _bf(macc_ref[...])
    for cc in range(n_col):
        cols = slice(cc * MXU_DIM, (cc + 1) * MXU_DIM)
        o_ref[:, cols] = x_ref[:, cols] + _dot(merged, wo_ref[:, cols])
    if final_norm:
        out = o_ref[...]
        ms = jnp.mean(out * out, axis=-1, keepdims=True)
        o_ref[...] = out * lax.rsqrt(ms + EPS) * fw_ref[...]


def _merge(x2, y_a, y_b, p, cos_t, sin_t, b_merge, w_branch_bf, w_out_bf, layer, final_w,
           final_norm, s):
    t, d = x2.shape
    w = BRANCH_W
    tm = min(MERGE_TM, s)
    blocks_per_seq = s // tm
    rows = lambda n: pl.BlockSpec((tm, n), lambda i: (i, 0))
    tok = lambda c: pl.BlockSpec((pl.Element(tm), pl.Element(w)),
                                 lambda i, c=c: (i * tm, P_OFF_C + c * w))
    tab =pl.BlockSpec((tm, RET_HEAD // 2), lambda i: (i % blocks_per_seq, 0))
    return pl.pallas_call(
        functools.partial(_merge_kernel, final_norm=final_norm, blocks_per_seq=blocks_per_seq),
        out_shape=jax.ShapeDtypeStruct((t, d), F32),
        grid=(t // tm,),
        in_specs=[rows(d), pl.BlockSpec((LRU_BLOCKS, tm, LRU_BLOCK), lambda i: (0, i, 0)),
                  rows(w), tok(0), tok(1), tok(2), tok(3), tab, tab,
                  pl.BlockSpec((tm, 3 * d), lambda i: (i, P_OFF_M // (3 * d))),
                  pl.BlockSpec((1, 3 * d), lambda i: (0, 0)),
                  pl.BlockSpec((None, 3, w, d), lambda i: (layer, 0, 0, 0)),
                  pl.BlockSpec((None, d, d), lambda i: (layer, 0, 0)),
                  pl.BlockSpec((1, d), lambda i: (0, 0))],
        out_specs=rows(d),
        scratch_shapes=[pltpu.VMEM((tm, w), BF16), pltpu.VMEM((tm, d), F32),
                        pltpu.VMEM((RET_HEADS, RET_HEAD, RET_HEAD), F32),
                        pltpu.VMEM((RET_HEADS, RET_CHUNK, RET_CHUNK), F32),
                        pltpu.VMEM((RET_HEADS, RET_CHUNK, RET_HEAD), F32),
                        pltpu.VMEM((RET_HEADS, RET_CHUNK, RET_HEAD), F32)],
        compiler_params=pltpu.CompilerParams(dimension_semantics=("arbitrary",),
                                             vmem_limit_bytes=MERGE_VMEM_LIMIT),
        name="merge_retention",
    )(x2, y_a, y_b, p, p, p, p, cos_t, sin_t, p, b_merge.reshape(1, 3 * d), w_branch_bf,
      w_out_bf, final_w.reshape(1, d))


def _rope_tables(s):
    half = RET_HEAD // 2
    inv_freq = 1.0 / (ROPE_BASE ** np.linspace(0.0, 1.0, half))
    ang = np.arange(s)[:, None] * inv_freq
    return jnp.asarray(np.cos(ang), F32), jnp.asarray(np.sin(ang), F32)


def _lora_block(decay_w2, iclr_a2):
    z = jnp.zeros((LORA, BRANCH_W), F32)
    return jnp.concatenate([jnp.concatenate([decay_w2, z], axis=1),
                            jnp.concatenate([z, iclr_a2], axis=1)], axis=0).astype(BF16)


def kernel(x, norm_w, w_in, b_merge, conv_w, conv_b, lru_gate_w, lru_gate_b, lru_lambda, shift_mu,
           decay_w0, decay_w2, iclr_a0, iclr_a2, k_k, k_a, r_k, lnx_w, lnx_b, w_branch, w_out,
           final_norm_w):
    bsz, s, d = x.shape
    depth = norm_w.shape[0]
    t = bsz * s
    cos_t, sin_t = _rope_tables(s)
    w_in_bf = w_in.astype(BF16)
    w_branch_bf = w_branch.astype(BF16)
    w_out_bf = w_out.astype(BF16)
    x2 = x.reshape(t, d)
    for l in range(depth):
        p, y_a = _proj_lru(x2, norm_w[l], w_in_bf, l, conv_w[l], conv_b[l],
                           lru_gate_w[l].astype(BF16), lru_gate_b[l], lru_lambda[l], s)
        p3 = p.reshape(bsz, s, P_COLS)
        y_b = _rwkv_branch(p3, shift_mu[l], _lora_block(decay_w2[l], iclr_a2[l]), decay_w0[l],
                           iclr_a0[l], k_k[l], k_a[l], r_k[l].reshape(-1), lnx_w[l], lnx_b[l])
        x2 = _merge(x2, y_a, y_b, p, cos_t, sin_t, b_merge[l], w_branch_bf, w_out_bf, l,
                    final_norm_w, l == depth - 1, s)
    return x2.reshape(bsz, s, d)
```

```python
import functools
import math

import jax
import jax.numpy as jnp
import numpy as np
from jax import lax
from jax.experimental import pallas as pl
from jax.experimental.pallas import tpu as pltpu

F32 = jnp.float32
BF16 = jnp.bfloat16

D_MODEL = 1024
EPS = 1e-6
BRANCH_W = 1024
LRU_BLOCKS = 4
LRU_BLOCK = BRANCH_W // LRU_BLOCKS
CONV_W = 4
LRU_C = 8.0
RWKV_HEAD = 64
LORA = 64
RWKV_GN_EPS = 64e-5
RET_HEADS = 4
RET_HEAD = BRANCH_W // RET_HEADS
RET_CHUNK = 128
ROPE_BASE = 10000.0
RET_GN_EPS = 1e-5
B_COLS = 4 * BRANCH_W + 2 * LORA
OFF_B = 2 * BRANCH_W
OFF_C = OFF_B + B_COLS
OFF_M = OFF_C + 4 * BRANCH_W
N_IN = OFF_M + 3 * D_MODEL
LANES = 128
SUBLANES = 8
MXU_DIM = 256
RWKV_CHUNK = 64
RWKV_TB = 256
PROJ_TM = 1024
LRU_PIECE_ROWS = 64
PROJ_TN = 3 * D_MODEL
P_COLS = PROJ_TN * LRU_BLOCKS
P_OFF_B = 0
P_OFF_C = OFF_C - OFF_B
P_OFF_M = P_COLS - PROJ_TN
assert OFF_B + (LRU_BLOCKS - 1) * PROJ_TN >= OFF_M and N_IN - OFF_M == PROJ_TN
VMEM_LIMIT = 48 * 1024 * 1024
MERGE_TM = 512
MERGE_VMEM_LIMIT = 56 * 1024 * 1024
PROJ_VMEM_LIMIT = 56 * 1024 * 1024


def _cparams(n_axes):
    return pltpu.CompilerParams(dimension_semantics=("arbitrary",) * n_axes,
                                vmem_limit_bytes=VMEM_LIMIT)


def _sigmoid(x):
    return 1.0 / (1.0 + jnp.exp(-x))


def _softplus(x):
    return jnp.maximum(x, 0.0) + jnp.log(1.0 + jnp.exp(-jnp.abs(x)))


def _dot(a, b):
    return jnp.dot(a, b, preferred_element_type=F32)


def _dot_nt(a, b):
    return lax.dot_general(a, b, (((1,), (1,)), ((), ())), preferred_element_type=F32)


def _bf(x):
    return x.astype(BF16)


def _split_hi_lo(x):
    hi = x.astype(BF16)
    lo = (x - hi.astype(F32)).astype(BF16)
    return hi, lo


def _lru_pieces(xg_ref, n, cw, cb, gw, gb, lam, tail, h, first, rows_per_piece, carry_out):
    tm, w = xg_ref.shape[2], xg_ref.shape[3]
    tail = jnp.where(first, 0.0, tail)
    h = jnp.where(first, 0.0, h)
    decay_rate = _softplus(-lam)
    sub = lax.broadcasted_iota(jnp.int32, (SUBLANES, w), 0)
    row = lax.broadcasted_iota(jnp.int32, (rows_per_piece, w), 0)
    for piece in range(tm // rows_per_piece):
        rows = slice(piece * rows_per_piece, (piece + 1) * rows_per_piece)
        x = xg_ref[0, n, rows, :]
        ga = xg_ref[1, n, rows, :]
        xfull = jnp.concatenate([tail, x], axis=0)
        tail = x[rows_per_piece - SUBLANES:, :]
        conv = cb + x * cw[CONV_W - 1:CONV_W, :]
        for s in range(1, CONV_W):
            shifted = pltpu.roll(xfull, s, 0)[SUBLANES:, :]
            conv = conv + shifted * cw[CONV_W - 1 - s:CONV_W - s, :]
        conv_bf = _bf(conv)
        r = _sigmoid(_dot(conv_bf, gw[0]) + gb[0:1, :])
        ig = _sigmoid(_dot(conv_bf, gw[1]) + gb[1:2, :])
        log_a = (-LRU_C) * r * decay_rate
        a = jnp.exp(log_a)
        z = -jnp.tanh(log_a) * (a * a + 1.0)
        mult = jnp.where(z > 0.0, z * lax.rsqrt(z), 0.0)
        if piece == 0:
            mult = jnp.where((row == 0) & first, 1.0, mult)
        b = mult * ig * conv

        tiles = []
        for g in range(rows_per_piece // SUBLANES):
            a8 = a[g * SUBLANES:(g + 1) * SUBLANES, :]
            b8 = b[g * SUBLANES:(g + 1) * SUBLANES, :]
            d = 1
            while d < SUBLANES:
                keep = sub >= d
                a_prev = jnp.where(keep, pltpu.roll(a8, d, 0), 1.0)
                b_prev = jnp.where(keep, pltpu.roll(b8, d, 0), 0.0)
                b8 = a8 * b_prev + b8
                a8 = a8 * a_prev
                d *= 2
            hs = a8 * h + b8
            tiles.append(hs)
            h = hs[SUBLANES - 1:SUBLANES, :]
        carry_out[:] = [tail, h]
        yield rows, jnp.concatenate(tiles, axis=0) * (ga * _sigmoid(ga))


def _proj_lru_kernel(x_ref, nw_ref, w_ref, wl_ref, cw_ref, cb_ref, gw_ref, gb_ref, lam_ref,
                     p_ref, ya_ref, h_ref, xg_ref, tail_ref, hc_ref, *, blocks_per_seq):
    i = pl.program_id(0)
    n = pl.program_id(1)

    @pl.when(n == 0)
    def _():
        x = x_ref[...]
        ms = jnp.mean(x * x, axis=-1, keepdims=True)
        h = _bf(x * lax.rsqrt(ms + EPS) * nw_ref[...])
        h_ref[...] = h
        xg = _dot(h, wl_ref[...])
        for half in range(2):
            for blk in range(LRU_BLOCKS):
                c0 = half * BRANCH_W + blk * LRU_BLOCK
                xg_ref[half, blk] = xg[:, c0:c0 + LRU_BLOCK]

    first = (i % blocks_per_seq) == 0
    carry = []
    pieces = _lru_pieces(xg_ref, n, cw_ref[n], cb_ref[n], gw_ref[n], gb_ref[n], lam_ref[n],
                         tail_ref[n], hc_ref[n], first, LRU_PIECE_ROWS, carry)
    h = h_ref[...]
    n_chunks = p_ref.shape[1] // MXU_DIM
    per_chunk = -(-(x_ref.shape[0] // LRU_PIECE_ROWS) // n_chunks)
    for c in range(n_chunks):
        cols = slice(c * MXU_DIM, (c + 1) * MXU_DIM)
        p_ref[:, cols] = _dot(h, w_ref[0, :, cols]).astype(p_ref.dtype)
        for _ in range(per_chunk):
            piece = next(pieces, None)
            if piece is not None:
                rows, y = piece
                ya_ref[n, rows, :] = y.astype(ya_ref.dtype)
    assert next(pieces, None) is None
    tail_ref[n], hc_ref[n] = carry


def _proj_lru(x2, norm_w, w_in_bf, layer, conv_w, conv_b, gate_w_bf, gate_b, lam, s):
    t, d = x2.shape
    overshoot = (OFF_B + (LRU_BLOCKS - 1) * PROJ_TN - (N_IN - PROJ_TN)) // LANES
    main_off = lambda j: (OFF_B // LANES + j * (PROJ_TN // LANES)
                          - overshoot * (j // (LRU_BLOCKS - 1))) * LANES
    tm = min(PROJ_TM, s)
    nblk, blk = LRU_BLOCKS, LRU_BLOCK
    per_block = lambda a: a.reshape(a.shape[0], nblk, blk).transpose(1, 0, 2)
    par = lambda shp: pl.BlockSpec(shp, lambda i, j: (0,) * len(shp))
    return pl.pallas_call(
        functools.partial(_proj_lru_kernel, blocks_per_seq=s // tm),
        out_shape=(jax.ShapeDtypeStruct((t, P_COLS), BF16),
                   jax.ShapeDtypeStruct((nblk, t, blk), BF16)),
        grid=(t // tm, nblk),
        in_specs=[pl.BlockSpec((tm, d), lambda i, j: (i, 0)),
                  par((1, d)),
                  pl.BlockSpec((pl.Element(1), pl.Element(d), pl.Element(PROJ_TN)),
                               lambda i, j: (layer, 0, main_off(j))),
                  pl.BlockSpec((None, d, OFF_B), lambda i, j: (layer, 0, 0)),
                  par((nblk, CONV_W, blk)), par((nblk, 1, blk)), par((nblk, 2, blk, blk)),
                  par((nblk, 2, blk)), par((nblk, 1, blk))],
        out_specs=(pl.BlockSpec((tm, PROJ_TN), lambda i, j: (i, j)),
                   pl.BlockSpec((nblk, tm, blk), lambda i, j: (0, i, 0))),
        scratch_shapes=[pltpu.VMEM((tm, d), BF16),
                        pltpu.VMEM((2, nblk, tm, blk), F32),
                        pltpu.VMEM((nblk, SUBLANES, blk), F32),
                        pltpu.VMEM((nblk, 1, blk), F32)],
        compiler_params=pltpu.CompilerParams(dimension_semantics=("arbitrary", "arbitrary"),
                                             vmem_limit_bytes=PROJ_VMEM_LIMIT),
        name="proj_rglru",
    )(x2, norm_w.reshape(1, d), w_in_bf, w_in_bf, per_block(conv_w),
      per_block(conv_b.reshape(1, -1)), gate_w_bf.transpose(1, 0, 2, 3), per_block(gate_b),
      per_block(lam.reshape(1, -1)))


def _seg64_sum(x, bd_ones):
    xb = _bf(x)
    out = [_dot(xb[:, j * MXU_DIM:(j + 1) * MXU_DIM], bd_ones) for j in range(x.shape[1] // MXU_DIM)]
    return jnp.concatenate(out, axis=1)


def _shift_rows(x, prev_row):
    rolled = pltpu.roll(x, 1, 0)
    head = rolled[0:SUBLANES, :]
    row = lax.broadcasted_iota(jnp.int32, head.shape, 0)
    head = jnp.where(row == 0, jnp.broadcast_to(prev_row, head.shape), head)
    return jnp.concatenate([head, rolled[SUBLANES:, :]], axis=0)


def _rwkv_kernel(r_ref, k_ref, v_ref, g_ref, lo_ref, mu_ref, mulo_ref, w2a_ref, w0_ref, a0_ref,
                 kk_ref, ka_ref, rk_ref, lnw_ref, lnb_ref, o_ref, prev_ref, prevlo_ref, state_ref):
    nbat, tb, w = r_ref.shape
    L = RWKV_CHUNK
    G = MXU_DIM
    n_chunks = tb // L
    n_groups = w // G
    hshift = RWKV_HEAD.bit_length() - 1

    @pl.when(pl.program_id(0) == 0)
    def _():
        prev_ref[...] = jnp.zeros_like(prev_ref)
        prevlo_ref[...] = jnp.zeros_like(prevlo_ref)
        state_ref[...] = jnp.zeros_like(state_ref)

    rg = lax.broadcasted_iota(jnp.int32, (G, G), 0)
    cg = lax.broadcasted_iota(jnp.int32, (G, G), 1)
    same_head = (rg >> hshift) == (cg >> hshift)
    bd_ones = same_head.astype(BF16)
    r2 = lax.broadcasted_iota(jnp.int32, (2 * L, G), 0)
    c2 = lax.broadcasted_iota(jnp.int32, (2 * L, G), 1) & (L - 1)
    score_mask = (c2 < jnp.where(r2 >= L, r2 - (L - 1), r2)).astype(BF16)
    re = lax.broadcasted_iota(jnp.int32, (L, G), 0)
    ce = lax.broadcasted_iota(jnp.int32, (L, G), 1) & (L - 1)
    strict_w = ce < re
    eye_w = (ce == re).astype(F32)
    rt = lax.broadcasted_iota(jnp.int32, (tb, tb), 0)
    ct = lax.broadcasted_iota(jnp.int32, (tb, tb), 1)
    tril_chunks = ((ct <= rt) & ((ct >> hshift) == (rt >> hshift))).astype(BF16)
    tril2 = jnp.concatenate([tril_chunks, tril_chunks], axis=1)
    assert L == RWKV_HEAD
    heads_per_tile = LANES // RWKV_HEAD
    lane_t = lax.broadcasted_iota(jnp.int32, (L, LANES), 1) >> hshift
    head_pick = [(lane_t == hh).astype(BF16) for hh in range(heads_per_tile)]
    tile_bd = bd_ones[0:LANES, 0:LANES]
    zero_lt = jnp.zeros((L, LANES), BF16)

    def bd(x):
        xb = _bf(x)
        cols = []
        for lt in range(G // LANES):
            tile = xb[:, lt * LANES:(lt + 1) * LANES]
            blocks = [zero_lt] * (G // L)
            for hh in range(heads_per_tile):
                blocks[lt * heads_per_tile + hh] = tile * head_pick[hh]
            cols.append(jnp.concatenate(blocks, axis=0))
        return jnp.concatenate(cols, axis=1)

    def token_shift(x, prev_row, mu_row):
        return x + (_shift_rows(x, prev_row) - x) * mu_row

    units = []
    post = []
    for bi in range(nbat):
        r_raw, k_raw, v_raw, g_raw, lo_raw = (ref[bi].astype(F32)
                                              for ref in (r_ref, k_ref, v_ref, g_ref, lo_ref))
        r = token_shift(r_raw, prev_ref[bi, 0:1, :], mu_ref[0:1, :])
        k = token_shift(k_raw, prev_ref[bi, 1:2, :], mu_ref[1:2, :])
        v = token_shift(v_raw, prev_ref[bi, 2:3, :], mu_ref[2:3, :])
        g = token_shift(g_raw, prev_ref[bi, 3:4, :], mu_ref[3:4, :])
        lo = token_shift(lo_raw, prevlo_ref[bi], mulo_ref[...])
        prev_ref[bi, 0:1, :] = r_raw[tb - 1:tb, :]
        prev_ref[bi, 1:2, :] = k_raw[tb - 1:tb, :]
        prev_ref[bi, 2:3, :] = v_raw[tb - 1:tb, :]
        prev_ref[bi, 3:4, :] = g_raw[tb - 1:tb, :]
        prevlo_ref[bi] = lo_raw[tb - 1:tb, :]

        lane_lo = lax.broadcasted_iota(jnp.int32, lo.shape, 1)
        lo_in = _bf(jnp.where(lane_lo < LORA, jnp.tanh(lo), lo))
        dwa = _dot(lo_in, w2a_ref[...])
        ld = (-math.exp(-0.5)) * _sigmoid(w0_ref[...] + dwa[:, :w])
        a = _sigmoid(a0_ref[...] + dwa[:, w:])

        kk = k * kk_ref[...]
        kk = kk * lax.rsqrt(jnp.maximum(_seg64_sum(kk * kk, bd_ones), 1e-24))
        kv = k * (1.0 + (a - 1.0) * ka_ref[...])
        b = kk * a

        cum = _dot(tril2, jnp.concatenate(_split_hi_lo(ld), axis=0))
        p_last = [jnp.exp(cum[(c + 1) * L - 1:(c + 1) * L, :]) for c in range(n_chunks)]
        p_inc = jnp.exp(cum)
        p_exc = jnp.exp(cum - ld)
        p_inv = jnp.exp(-cum)
        p_hat = p_inv * jnp.concatenate([jnp.broadcast_to(pl_, (L, w)) for pl_ in p_last], axis=0)

        kap_t = kk * p_exc
        r_t = r * p_inc
        b_t = b * p_inv
        k_t = kv * p_inv
        b_h = b * p_hat
        k_h = kv * p_hat
        post.append((r, kv, v, g))
        for c in range(n_chunks):
            rows = slice(c * L, (c + 1) * L)
            for gi in range(n_groups):
                cols = slice(gi * G, (gi + 1) * G)
                units.append(dict(
                    c=c, bi=bi, gi=gi,
                    gq=_bf(jnp.concatenate([kap_t[rows, cols], r_t[rows, cols]], axis=0)),
                    b_t=b_t[rows, cols], k_t=k_t[rows, cols], v=v[rows, cols],
                    zh=jnp.concatenate([b_h[rows, cols], k_h[rows, cols]], axis=0),
                    p_last=p_last[c][:, cols]))

    for u in units:
        scb = _dot_nt(u["gq"], bd(u["b_t"]))
        sck = _dot_nt(u["gq"], bd(u["k_t"]))
        u["pw"] = jnp.where(strict_w, -scb[0:L], 0.0)
        u["r_b"] = _bf(scb[L:2 * L]) * score_mask[L:2 * L]
        u["ak_rk"] = _bf(sck) * score_mask
        u["tinv"] = eye_w + u["pw"]
    for u in units:
        akrk = _dot(u["ak_rk"], bd(u["v"]))
        u["akv"] = akrk[0:L]
        u["rkv"] = akrk[L:2 * L]
        u["zh_t"] = _bf(u["zh"].T)
        u["p_col"] = jnp.broadcast_to(u["p_last"], (LANES, G)).T
    for u in units:
        u["pw_next"] = _dot(_bf(u["pw"]), bd(u["pw"]))
    for _ in range(int(math.log2(L)) - 2):
        for u in units:
            u["pw"] = u["pw_next"]
            both = _dot(_bf(jnp.concatenate([u["tinv"], u["pw"]], axis=0)), bd(u["pw"]))
            u["tinv"] = u["tinv"] + both[0:L]
            u["pw_next"] = both[L:2 * L]
    for u in units:
        u["tinv"] = u["tinv"] + _dot(_bf(u["tinv"]), bd(u["pw_next"]))

    ys = {}
    n_q = G // LANES
    zero_q = jnp.zeros((LANES, LANES), BF16)
    for c in range(n_chunks):
        cu = [u for u in units if u["c"] == c]
        for u in cu:
            st = [state_ref[u["bi"], u["gi"], q] for q in range(n_q)]
            u["st"] = st
            st_bd = jnp.concatenate(
                [jnp.concatenate([_bf(st[q]) * tile_bd if qq == q else zero_q for qq in range(n_q)],
                                 axis=1) for q in range(n_q)], axis=0)
            x0 = _dot(u["gq"], st_bd)
            u["rhs"] = -(x0[0:L] + u["akv"])
            u["y"] = x0[L:2 * L] + u["rkv"]
        for u in cu:
            u["u"] = _dot(_bf(u["tinv"]), bd(u["rhs"]))
        for u in cu:
            ys[(u["bi"], c, u["gi"])] = u["y"] + _dot(u["r_b"], bd(u["u"]))
            uv = _bf(jnp.concatenate([u["u"], u["v"]], axis=0))
            for q in range(n_q):
                qs = slice(q * LANES, (q + 1) * LANES)
                state_ref[u["bi"], u["gi"], q] = (u["st"][q] * u["p_col"][qs, :]
                                                  + _dot(u["zh_t"][qs, :], uv[:, qs]))

    for bi in range(nbat):
        r, kv, v, g = post[bi]
        y = jnp.concatenate(
            [jnp.concatenate([ys[(bi, c, gi)] for gi in range(n_groups)], axis=1)
             for c in range(n_chunks)], axis=0)
        mean = _seg64_sum(y, bd_ones) * (1.0 / RWKV_HEAD)
        d = y - mean
        var = _seg64_sum(d * d, bd_ones) * (1.0 / RWKV_HEAD)
        yn = d * lax.rsqrt(var + RWKV_GN_EPS) * lnw_ref[...] + lnb_ref[...]
        bonus = _seg64_sum(r * kv * rk_ref[...], bd_ones) * v
        o_ref[bi] = ((yn + bonus) * (g * _sigmoid(g))).astype(o_ref.dtype)


def _rwkv_branch(p3, mu, w2a_bf, w0, a0, k_k, k_a, r_k, lnx_w, lnx_b):
    bsz, s, _ = p3.shape
    w = BRANCH_W
    tb = min(RWKV_TB, s)
    nb = s // tb
    row = lambda x: x.reshape(1, -1)
    mu4 = mu[:4 * w].reshape(4, w)
    mulo = mu[4 * w:].reshape(1, 2 * LORA)
    tok = lambda c: pl.BlockSpec((bsz, tb, w), lambda i, c=c: (0, i, P_OFF_B // w + c))
    par = lambda shp: pl.BlockSpec(shp, lambda i: (0,) * len(shp))
    out = pl.pallas_call(
        _rwkv_kernel,
        out_shape=jax.ShapeDtypeStruct((bsz, s, w), BF16),
        grid=(nb,),
        in_specs=[tok(0), tok(1), tok(2), tok(3),
                  pl.BlockSpec((bsz, tb, 2 * LORA), lambda i: (0, i, (P_OFF_B + 4 * w) // (2 * LORA))),
                  par((4, w)), par((1, 2 * LORA)), par((2 * LORA, 2 * w)),
                  par((1, w)), par((1, w)), par((1, w)), par((1, w)), par((1, w)),
                  par((1, w)), par((1, w))],
        out_specs=pl.BlockSpec((bsz, tb, w), lambda i: (0, i, 0)),
        scratch_shapes=[pltpu.VMEM((bsz, 4, w), F32), pltpu.VMEM((bsz, 1, 2 * LORA), F32),
                        pltpu.VMEM((bsz, w // MXU_DIM, MXU_DIM // LANES, LANES, LANES), F32)],
        compiler_params=_cparams(1),
        name="rwkv7",
    )(p3, p3, p3, p3, p3, mu4, mulo, w2a_bf, row(w0), row(a0), row(k_k), row(k_a),
      row(r_k), row(lnx_w), row(lnx_b))
    return out.reshape(bsz * s, w)


RET_LOG_G = [math.log1p(-(2.0 ** (-5.0 - h))) for h in range(RET_HEADS)]


def _ret_tables(mask_ref, qdec_ref, kdec_ref):
    C, dh = RET_CHUNK, RET_HEAD
    ri = lax.broadcasted_iota(jnp.int32, (C, C), 0)
    ci = lax.broadcasted_iota(jnp.int32, (C, C), 1)
    rel = (ri - ci).astype(F32)
    rowf = lax.broadcasted_iota(jnp.int32, (C, dh), 0).astype(F32)
    for h in range(RET_HEADS):
        mask_ref[h] = jnp.where(rel >= 0.0, jnp.exp(RET_LOG_G[h] * jnp.maximum(rel, 0.0)),
                                0.0) * (dh ** -0.5)
        qdec_ref[h] = jnp.exp(RET_LOG_G[h] * (rowf + 1.0))
        kdec_ref[h] = jnp.exp(RET_LOG_G[h] * (C - 1.0 - rowf)) * (dh ** -0.5)


def _ret_pieces(q_ref, k_ref, v_ref, g_ref, cos_ref, sin_ref, yc_ref, state_ref, mask_ref,
                qdec_ref, kdec_ref, first):
    C, dh = RET_CHUNK, RET_HEAD
    half = dh // 2
    for c in range(q_ref.shape[0] // C):
        rows = slice(c * C, (c + 1) * C)
        cos = cos_ref[rows, :]
        sin = sin_ref[rows, :]

        def rot(t):
            t1, t2 = t[:, :half], t[:, half:]
            return jnp.concatenate([t1 * cos - t2 * sin, t1 * sin + t2 * cos], axis=1)

        for h in range(RET_HEADS):
            sl = slice(h * dh, (h + 1) * dh)
            q = rot(q_ref[rows, sl].astype(F32))
            k = rot(k_ref[rows, sl].astype(F32))
            v_bf = v_ref[rows, sl]
            scores = _dot_nt(_bf(q), _bf(k)) * mask_ref[h]
            inner = _dot(_bf(scores), v_bf)
            st = state_ref[h]
            if c == 0:
                st = jnp.where(first, 0.0, st)
            cross = _dot(_bf(q * qdec_ref[h]), _bf(st))
            state_ref[h] = st * math.exp(RET_LOG_G[h] * C) + _dot(_bf((k * kdec_ref[h]).T), v_bf)
            out = inner + cross
            mu = jnp.mean(out, axis=-1, keepdims=True)
            dlt = out - mu
            var = jnp.mean(dlt * dlt, axis=-1, keepdims=True)
            g = g_ref[rows, sl].astype(F32)
            yc_ref[rows, sl] = (dlt * lax.rsqrt(var + RET_GN_EPS) * (g * _sigmoid(g))).astype(
                yc_ref.dtype)
            yield


def _merge_kernel(x_ref, ya_ref, yb_ref, q_ref, k_ref, v_ref, g_ref, cos_ref, sin_ref, pm_ref,
                  bm_ref, wb_ref, wo_ref, fw_ref, o_ref, yc_ref, macc_ref, state_ref, mask_ref,
                  qdec_ref, kdec_ref, *, final_norm, blocks_per_seq):
    i = pl.program_id(0)
    d = x_ref.shape[1]
    n_col = d // MXU_DIM

    @pl.when(i == 0)
    def _():
        state_ref[...] = jnp.zeros_like(state_ref)
        _ret_tables(mask_ref, qdec_ref, kdec_ref)

    def gate(n, cols):
        off = n * d
        return _sigmoid(pm_ref[:, off + cols.start:off + cols.stop].astype(F32)
                        + bm_ref[:, off + cols.start:off + cols.stop])

    pieces = _ret_pieces(q_ref, k_ref, v_ref, g_ref, cos_ref, sin_ref, yc_ref, state_ref,
                         mask_ref, qdec_ref, kdec_ref, (i % blocks_per_seq) == 0)
    n_pieces = (x_ref.shape[0] // RET_CHUNK) * RET_HEADS
    per_slot = -(-n_pieces // (2 * n_col))

    def run_pieces():
        for _ in range(per_slot):
            next(pieces, None)

    for cc in range(n_col):
        cols = slice(cc * MXU_DIM, (cc + 1) * MXU_DIM)
        pa = sum(_dot(ya_ref[blk], wb_ref[0, blk * LRU_BLOCK:(blk + 1) * LRU_BLOCK, cols])
                 for blk in range(LRU_BLOCKS))
        macc_ref[:, cols] = gate(0, cols) * pa
        run_pieces()
        macc_ref[:, cols] += gate(1, cols) * _dot(yb_ref[...], wb_ref[1, :, cols])
        run_pieces()
    assert next(pieces, None) is None

    yc = yc_ref[...]
    for cc in range(n_col):
        cols = slice(cc * MXU_DIM, (cc + 1) * MXU_DIM)
        macc_ref[:, cols] += gate(2, cols) * _dot(yc, wb_ref[2, :, cols])
    merged = _bf(macc_ref[...])
    for cc in range(n_col):
        cols = slice(cc * MXU_DIM, (cc + 1) * MXU_DIM)
        o_ref[:, cols] = x_ref[:, cols] + _dot(merged, wo_ref[:, cols])
    if final_norm:
        out = o_ref[...]
        ms = jnp.mean(out * out, axis=-1, keepdims=True)
        o_ref[...] = out * lax.rsqrt(ms + EPS) * fw_ref[...]


def _merge(x2, y_a, y_b, p, cos_t, sin_t, b_merge, w_branch_bf, w_out_bf, layer, final_w,
           final_norm, s):
    t, d = x2.shape
    w = BRANCH_W
    tm = min(MERGE_TM, s)
    blocks_per_seq = s // tm
    rows = lambda n: pl.BlockSpec((tm, n), lambda i: (i, 0))
    tok = lambda c: pl.BlockSpec((pl.Element(tm), pl.Element(w)),
                                 lambda i, c=c: (i * tm, P_OFF_C + c * w))
    tab =pl.BlockSpec((tm, RET_HEAD // 2), lambda i: (i % blocks_per_seq, 0))
    return pl.pallas_call(
        functools.partial(_merge_kernel, final_norm=final_norm, blocks_per_seq=blocks_per_seq),
        out_shape=jax.ShapeDtypeStruct((t, d), F32),
        grid=(t // tm,),
        in_specs=[rows(d), pl.BlockSpec((LRU_BLOCKS, tm, LRU_BLOCK), lambda i: (0, i, 0)),
                  rows(w), tok(0), tok(1), tok(2), tok(3), tab, tab,
                  pl.BlockSpec((tm, 3 * d), lambda i: (i, P_OFF_M // (3 * d))),
                  pl.BlockSpec((1, 3 * d), lambda i: (0, 0)),
                  pl.BlockSpec((None, 3, w, d), lambda i: (layer, 0, 0, 0)),
                  pl.BlockSpec((None, d, d), lambda i: (layer, 0, 0)),
                  pl.BlockSpec((1, d), lambda i: (0, 0))],
        out_specs=rows(d),
        scratch_shapes=[pltpu.VMEM((tm, w), BF16), pltpu.VMEM((tm, d), F32),
                        pltpu.VMEM((RET_HEADS, RET_HEAD, RET_HEAD), F32),
                        pltpu.VMEM((RET_HEADS, RET_CHUNK, RET_CHUNK), F32),
                        pltpu.VMEM((RET_HEADS, RET_CHUNK, RET_HEAD), F32),
                        pltpu.VMEM((RET_HEADS, RET_CHUNK, RET_HEAD), F32)],
        compiler_params=pltpu.CompilerParams(dimension_semantics=("arbitrary",),
                                             vmem_limit_bytes=MERGE_VMEM_LIMIT),
        name="merge_retention",
    )(x2, y_a, y_b, p, p, p, p, cos_t, sin_t, p, b_merge.reshape(1, 3 * d), w_branch_bf,
      w_out_bf, final_w.reshape(1, d))


def _rope_tables(s):
    half = RET_HEAD // 2
    inv_freq = 1.0 / (ROPE_BASE ** np.linspace(0.0, 1.0, half))
    ang = np.arange(s)[:, None] * inv_freq
    return jnp.asarray(np.cos(ang), F32), jnp.asarray(np.sin(ang), F32)


def _lora_block(decay_w2, iclr_a2):
    z = jnp.zeros((LORA, BRANCH_W), F32)
    return jnp.concatenate([jnp.concatenate([decay_w2, z], axis=1),
                            jnp.concatenate([z, iclr_a2], axis=1)], axis=0).astype(BF16)


def kernel(x, norm_w, w_in, b_merge, conv_w, conv_b, lru_gate_w, lru_gate_b, lru_lambda, shift_mu,
           decay_w0, decay_w2, iclr_a0, iclr_a2, k_k, k_a, r_k, lnx_w, lnx_b, w_branch, w_out,
           final_norm_w):
    bsz, s, d = x.shape
    depth = norm_w.shape[0]
    t = bsz * s
    cos_t, sin_t = _rope_tables(s)
    w_in_bf = w_in.astype(BF16)
    w_branch_bf = w_branch.astype(BF16)
    w_out_bf = w_out.astype(BF16)
    x2 = x.reshape(t, d)
    for l in range(depth):
        p, y_a = _proj_lru(x2, norm_w[l], w_in_bf, l, conv_w[l], conv_b[l],
                           lru_gate_w[l].astype(BF16), lru_gate_b[l], lru_lambda[l], s)
        p3 = p.reshape(bsz, s, P_COLS)
        y_b = _rwkv_branch(p3, shift_mu[l], _lora_block(decay_w2[l], iclr_a2[l]), decay_w0[l],
                           iclr_a0[l], k_k[l], k_a[l], r_k[l].reshape(-1), lnx_w[l], lnx_b[l])
        x2 = _merge(x2, y_a, y_b, p, cos_t, sin_t, b_merge[l], w_branch_bf, w_out_bf, l,
                    final_norm_w, l == depth - 1, s)
    return x2.reshape(bsz, s, d)
```
